```python
import jax, jax.numpy as jnp
from jax import lax
import numpy as np

D_MODEL = 2048
BATCH = 2
SEQ = 16384
DEPTH = 1
DEC_BATCH = 4
DEC_SEQ = 2048
PAST_LEN = 128

N_ATTN_HEADS = 8
QK_NOPE_DIM = 128
QK_ROPE_DIM = 64
QK_HEAD_DIM = QK_NOPE_DIM + QK_ROPE_DIM
V_HEAD_DIM = 128
ATTN_WIDTH = N_ATTN_HEADS * V_HEAD_DIM
Q_LORA_RANK = 512
KV_LORA_RANK = 256
ROPE_BASE = 10000.0
Q_BLOCK = 128
LRU_WIDTH = D_MODEL - ATTN_WIDTH
LRU_BLOCKS = 8
LRU_BLOCK_DIM = LRU_WIDTH // LRU_BLOCKS
CONV_WIDTH = 4
CONV_PAD_LEFT = 1
CONV_PAD_RIGHT = CONV_WIDTH - 1 - CONV_PAD_LEFT
LRU_C = 8.0
MIX_WIDTH = ATTN_WIDTH + LRU_WIDTH
D_FF = -(-8 * D_MODEL // (3 * 256)) * 256
EPS = 1e-6
OFF_Q = Q_LORA_RANK
OFF_KV = OFF_Q + KV_LORA_RANK
OFF_PE = OFF_KV + QK_ROPE_DIM
OFF_LX = OFF_PE + LRU_WIDTH
IN_COLS = OFF_LX + LRU_WIDTH

kernel_name = "hymba_mla_rglru_encoder"


def rms_norm(x, g):
    xf = x.astype(jnp.float32)
    y = xf * lax.rsqrt(jnp.mean(xf * xf, axis=-1, keepdims=True) + EPS)
    return (y * g.astype(jnp.float32)).astype(x.dtype)


def rope(x, positions):
    half = QK_ROPE_DIM // 2
    inv_freq = ROPE_BASE ** (-2.0 * jnp.arange(half, dtype=jnp.float32) / QK_ROPE_DIM)
    ang = positions[:, None] * inv_freq[None, :]
    cos = jnp.cos(ang)[:, None, :]
    sin = jnp.sin(ang)[:, None, :]
    xf = x.astype(jnp.float32)
    x1, x2 = xf[..., :half], xf[..., half:]
    out = jnp.concatenate([x1 * cos - x2 * sin, x2 * cos + x1 * sin], axis=-1)
    return out.astype(x.dtype)


def block_attention(q, k, v):
    B, S, H, Dk = q.shape
    nb = S // Q_BLOCK
    qb = q.reshape(B, nb, Q_BLOCK, H, Dk).transpose(1, 0, 2, 3, 4)
    scale = QK_HEAD_DIM ** -0.5

    def one_block(qi):
        s = jnp.einsum('bqhd,bkhd->bhqk', qi, k, preferred_element_type=jnp.float32) * scale
        p = jax.nn.softmax(s, axis=-1)
        return jnp.einsum('bhqk,bkhd->bqhd', p.astype(v.dtype), v)

    o = lax.map(one_block, qb)
    return o.transpose(1, 0, 2, 3, 4).reshape(B, S, H, V_HEAD_DIM)


def mla_mixer(c_q, c_kv, k_pe, positions, q_a_norm_g, w_uq, kv_a_norm_g, w_ukv,
              q_norm_g, k_norm_g):
    B, S, _ = c_q.shape
    q = (rms_norm(c_q, q_a_norm_g) @ w_uq).reshape(B, S, N_ATTN_HEADS, QK_HEAD_DIM)
    kv = (rms_norm(c_kv, kv_a_norm_g) @ w_ukv).reshape(B, S, N_ATTN_HEADS, QK_NOPE_DIM + V_HEAD_DIM)
    k_nope, v = kv[..., :QK_NOPE_DIM], kv[..., QK_NOPE_DIM:]
    k_pe_h = jnp.broadcast_to(k_pe[:, :, None, :], (B, S, N_ATTN_HEADS, QK_ROPE_DIM))
    k = jnp.concatenate([k_nope, k_pe_h], axis=-1)
    q = rms_norm(q, q_norm_g)
    k = rms_norm(k, k_norm_g)
    q = jnp.concatenate([q[..., :QK_NOPE_DIM], rope(q[..., QK_NOPE_DIM:], positions)], axis=-1)
    k = jnp.concatenate([k[..., :QK_NOPE_DIM], rope(k[..., QK_NOPE_DIM:], positions)], axis=-1)
    o = block_attention(q, k, v)
    return o.reshape(B, S, ATTN_WIDTH)


def centred_depthwise_conv(x, w, b):
    S = x.shape[1]
    xp = jnp.pad(x, ((0, 0), (CONV_PAD_LEFT, CONV_PAD_RIGHT), (0, 0)))
    y = b
    for t in range(CONV_WIDTH):
        y = y + xp[:, t:t + S] * w[t]
    return y


def block_diag(x, w, b):
    B, S, _ = x.shape
    xb = x.reshape(B, S, LRU_BLOCKS, LRU_BLOCK_DIM)
    return (jnp.einsum('bsnc,ncd->bsnd', xb, w) + b).reshape(B, S, LRU_WIDTH)


def linear_scan(a, bx, reverse):
    def combine(l, r):
        a_l, b_l = l
        a_r, b_r = r
        return a_l * a_r, a_r * b_l + b_r
    _, h = lax.associative_scan(combine, (a, bx), axis=1, reverse=reverse)
    return h


def rg_lru_direction(xf, w_r, b_r, w_i, b_i, lam, reverse):
    f32 = jnp.float32
    r = jax.nn.sigmoid(block_diag(xf, w_r.astype(f32), b_r.astype(f32)))
    i = jax.nn.sigmoid(block_diag(xf, w_i.astype(f32), b_i.astype(f32)))
    log_a = -LRU_C * r * jax.nn.softplus(-lam.astype(f32))
    a = jnp.exp(log_a)
    mult = jnp.sqrt(-jnp.expm1(2.0 * log_a))
    return linear_scan(a, mult * (i * xf), reverse)


def rglru_mixer(u_x, u_gate, conv_w, conv_b, w_rg_r, b_rg_r, w_rg_i, b_rg_i, lru_lambda):
    xc = centred_depthwise_conv(u_x, conv_w, conv_b).astype(jnp.float32)
    h = (rg_lru_direction(xc, w_rg_r[0], b_rg_r[0], w_rg_i[0], b_rg_i[0], lru_lambda[0], False)
         + rg_lru_direction(xc, w_rg_r[1], b_rg_r[1], w_rg_i[1], b_rg_i[1], lru_lambda[1], True))
    return (h * jax.nn.gelu(u_gate.astype(jnp.float32))).astype(u_x.dtype)


def encoder_layer(x, attn_norm_g, w_in, q_a_norm_g, w_uq, kv_a_norm_g, w_ukv, q_norm_g,
                  k_norm_g, conv_w, conv_b, w_rg_r, b_rg_r, w_rg_i, b_rg_i, lru_lambda,
                  attn_out_norm_g, lru_out_norm_g, w_out, ffn_norm_g, w_gate, w_up, w_down):
    S = x.shape[1]
    positions = jnp.arange(S, dtype=jnp.float32)
    h = rms_norm(x, attn_norm_g)
    u = h @ w_in
    c_q = u[..., :OFF_Q]
    c_kv = u[..., OFF_Q:OFF_KV]
    k_pe = u[..., OFF_KV:OFF_PE]
    u_x = u[..., OFF_PE:OFF_LX]
    u_gate = u[..., OFF_LX:]
    attn = mla_mixer(c_q, c_kv, k_pe, positions, q_a_norm_g, w_uq, kv_a_norm_g, w_ukv,
                     q_norm_g, k_norm_g)
    lru = rglru_mixer(u_x, u_gate, conv_w, conv_b, w_rg_r, b_rg_r, w_rg_i, b_rg_i,
                      lru_lambda)
    mixed = jnp.concatenate([rms_norm(attn, attn_out_norm_g), rms_norm(lru, lru_out_norm_g)], axis=-1)
    x = x + mixed @ w_out
    h = rms_norm(x, ffn_norm_g)
    x = x + (jax.nn.silu(h @ w_gate) * (h @ w_up)) @ w_down
    return x


def setup_inputs(seed: int = 0) -> dict:
    key = jax.random.key(seed)
    ks = jax.random.split(key, 32)
    f32 = jnp.float32

    def dense(k, shape, fan_in):
        return jax.random.normal(k, shape, f32) * fan_in ** -0.5

    def gain(k, n):
        return 1.0 + 0.01 * jax.random.normal(k, (DEPTH, n), f32)

    def small(k, shape):
        return 0.01 * jax.random.normal(k, shape, f32)

    u = jax.random.uniform(ks[18], (DEPTH, 2, LRU_WIDTH), f32, minval=0.9, maxval=0.999)
    return {
        "x_prompt": jax.random.normal(ks[0], (BATCH, SEQ, D_MODEL), f32),
        "x_sample": jax.random.normal(ks[1], (DEC_BATCH, DEC_SEQ, D_MODEL), f32),
        "attn_norm_g": gain(ks[2], D_MODEL),
        "w_in": dense(ks[3], (DEPTH, D_MODEL, IN_COLS), D_MODEL),
        "q_a_norm_g": gain(ks[4], Q_LORA_RANK),
        "w_uq": dense(ks[5], (DEPTH, Q_LORA_RANK, N_ATTN_HEADS * QK_HEAD_DIM), Q_LORA_RANK),
        "kv_a_norm_g": gain(ks[6], KV_LORA_RANK),
        "w_ukv": dense(ks[7], (DEPTH, KV_LORA_RANK, N_ATTN_HEADS * (QK_NOPE_DIM + V_HEAD_DIM)), KV_LORA_RANK),
        "q_norm_g": gain(ks[8], QK_HEAD_DIM),
        "k_norm_g": gain(ks[9], QK_HEAD_DIM),
        "conv_w": dense(ks[10], (DEPTH, CONV_WIDTH, LRU_WIDTH), CONV_WIDTH),
        "conv_b": small(ks[11], (DEPTH, LRU_WIDTH)),
        "w_rg_r": dense(ks[12], (DEPTH, 2, LRU_BLOCKS, LRU_BLOCK_DIM, LRU_BLOCK_DIM), LRU_BLOCK_DIM),
        "b_rg_r": small(ks[13], (DEPTH, 2, LRU_BLOCKS, LRU_BLOCK_DIM)),
        "w_rg_i": dense(ks[14], (DEPTH, 2, LRU_BLOCKS, LRU_BLOCK_DIM, LRU_BLOCK_DIM), LRU_BLOCK_DIM),
        "b_rg_i": small(ks[15], (DEPTH, 2, LRU_BLOCKS, LRU_BLOCK_DIM)),
        "lru_lambda": jnp.log(u) - jnp.log1p(-u),
        "attn_out_norm_g": gain(ks[16], ATTN_WIDTH),
        "lru_out_norm_g": gain(ks[17], LRU_WIDTH),
        "w_out": dense(ks[19], (DEPTH, MIX_WIDTH, D_MODEL), MIX_WIDTH),
        "ffn_norm_g": gain(ks[20], D_MODEL),
        "w_gate": dense(ks[21], (DEPTH, D_MODEL, D_FF), D_MODEL),
        "w_up": dense(ks[22], (DEPTH, D_MODEL, D_FF), D_MODEL),
        "w_down": dense(ks[23], (DEPTH, D_FF, D_MODEL), D_FF),
    }


def reference(x_prompt, x_sample, attn_norm_g, w_in, q_a_norm_g, w_uq, kv_a_norm_g, w_ukv,
              q_norm_g, k_norm_g, conv_w, conv_b, w_rg_r, b_rg_r, w_rg_i, b_rg_i, lru_lambda,
              attn_out_norm_g, lru_out_norm_g, w_out, ffn_norm_g, w_gate, w_up, w_down):
    y_prompt = x_prompt
    y_sample = x_sample
    for l in range(DEPTH):
        p = (attn_norm_g[l], w_in[l], q_a_norm_g[l], w_uq[l], kv_a_norm_g[l], w_ukv[l],
             q_norm_g[l], k_norm_g[l], conv_w[l], conv_b[l], w_rg_r[l], b_rg_r[l],
             w_rg_i[l], b_rg_i[l], lru_lambda[l], attn_out_norm_g[l], lru_out_norm_g[l],
             w_out[l], ffn_norm_g[l], w_gate[l], w_up[l], w_down[l])
        y_prompt = encoder_layer(y_prompt, *p)
        y_sample = encoder_layer(y_sample, *p)
    return (y_prompt, y_sample)
```

```python
import functools

import jax
import jax.numpy as jnp
from jax import lax
from jax.experimental import pallas as pl
from jax.experimental.pallas import tpu as pltpu

D_MODEL = 2048
N_HEADS = 8
NOPE = 128
ROPE = 64
QK_DIM = NOPE + ROPE
V_DIM = 128
ATTN_WIDTH = N_HEADS * V_DIM
Q_RANK = 512
KV_RANK = 256
ROPE_BASE = 10000.0
LRU_WIDTH = D_MODEL - ATTN_WIDTH
LRU_BLOCKS = 8
LRU_BD = LRU_WIDTH // LRU_BLOCKS
CONV_W = 4
LRU_C = 8.0
D_FF = 5632
EPS = 1e-6

LANES = 128
SUBLANES = 8
HEAD_PAD = 2 * LANES
OFF_Q = Q_RANK
OFF_KV = OFF_Q + KV_RANK
OFF_KG = OFF_KV + LANES
OFF_UX = OFF_KG + LRU_WIDTH
IN_COLS_EXT = OFF_UX + LRU_WIDTH
VMEM_LIMIT = 56 * 1024 * 1024

_F32 = jnp.float32
_BF16 = jnp.bfloat16


def _const_spec(shape):
    return pl.BlockSpec(shape, lambda *_: (0,) * len(shape), pipeline_mode=pl.Buffered(1))


def _params(sem):
    return pltpu.CompilerParams(dimension_semantics=sem, vmem_limit_bytes=VMEM_LIMIT)


def _rms(x, g):
    return x * lax.rsqrt(jnp.mean(x * x, axis=-1, keepdims=True) + EPS) * g


def _in_proj_kernel(x_ref, cos_ref, sin_ref, g_in_ref, w_in_ref, g_qa_ref, w_uq_ref,
                    g_kva_ref, w_uk_ref, w_uv_ref, gq_ref, gk_ref,
                    q_ref, k_ref, v_ref, ux_ref, ug_ref):
    x = x_ref[0]
    h = _rms(x, g_in_ref[...]).astype(_BF16)
    u = jnp.dot(h, w_in_ref[...], preferred_element_type=_F32)
    ux_ref[0] = u[:, OFF_KG:OFF_UX]
    ug_ref[0] = u[:, OFF_UX:]

    cos = cos_ref[...]
    sin = sin_ref[...]
    lane = lax.broadcasted_iota(jnp.int32, (1, LANES), 1)
    rope_mask = (lane < ROPE).astype(_F32)
    scale = QK_DIM ** -0.5

    def rotary(grp, g_rows):
        return grp * (g_rows[1:2] * cos) + pltpu.roll(grp, ROPE, axis=1) * (g_rows[2:3] * sin)

    def head_rms(nope, grp, extra=None):
        ss = jnp.sum(nope * nope, axis=-1, keepdims=True)
        ss = ss + (jnp.sum(grp * grp * rope_mask, axis=-1, keepdims=True) if extra is None else extra)
        return lax.rsqrt(ss * (1.0 / QK_DIM) + EPS)

    cq = _rms(u[:, :OFF_Q], g_qa_ref[...]).astype(_BF16)
    qf = jnp.dot(cq, w_uq_ref[...], preferred_element_type=_F32)
    gq = gq_ref[...]
    for hd in range(N_HEADS):
        nope = qf[:, hd * HEAD_PAD:hd * HEAD_PAD + NOPE]
        grp = qf[:, hd * HEAD_PAD + NOPE:(hd + 1) * HEAD_PAD]
        r = head_rms(nope, grp) * scale
        q_ref[0, hd, :, :NOPE] = (nope * r * gq[0:1]).astype(_BF16)
        q_ref[0, hd, :, NOPE:] = (rotary(grp, gq) * r).astype(_BF16)

    ckv = _rms(u[:, OFF_Q:OFF_KV], g_kva_ref[...]).astype(_BF16)
    v_ref[0] = jnp.dot(ckv, w_uv_ref[...], preferred_element_type=_F32).astype(_BF16)
    kn = jnp.dot(ckv, w_uk_ref[...], preferred_element_type=_F32)
    gk = gk_ref[...]
    kgrp = u[:, OFF_KV:OFF_KG]
    ss_pe = jnp.sum(kgrp * kgrp * rope_mask, axis=-1, keepdims=True)
    krot = rotary(kgrp, gk)
    for hd in range(N_HEADS):
        nope = kn[:, hd * NOPE:(hd + 1) * NOPE]
        r = head_rms(nope, None, ss_pe)
        k_ref[0, hd, :, :NOPE] = (nope * r * gk[0:1]).astype(_BF16)
        k_ref[0, hd, :, NOPE:] = (krot * r).astype(_BF16)


def _in_proj(x, cos, sin, g_in, w_in_ext, g_qa, w_uq_ext, g_kva, w_uk, w_uv, gq, gk, tm):
    B, S, _ = x.shape
    grid = (B, S // tm)
    tok = lambda w: pl.BlockSpec((1, tm, w), lambda b, i: (b, i, 0))
    head = pl.BlockSpec((1, N_HEADS, tm, HEAD_PAD), lambda b, i: (b, 0, i, 0))
    tab = pl.BlockSpec((tm, LANES), lambda b, i: (i, 0))
    return pl.pallas_call(
        _in_proj_kernel,
        grid=grid,
        in_specs=[tok(D_MODEL), tab, tab,
                  _const_spec((1, D_MODEL)), _const_spec((D_MODEL, IN_COLS_EXT)),
                  _const_spec((1, Q_RANK)), _const_spec((Q_RANK, N_HEADS * HEAD_PAD)),
                  _const_spec((1, KV_RANK)), _const_spec((KV_RANK, N_HEADS * NOPE)),
                  _const_spec((KV_RANK, ATTN_WIDTH)),
                  _const_spec((3, LANES)), _const_spec((3, LANES))],
        out_specs=[head, head, tok(ATTN_WIDTH), tok(LRU_WIDTH), tok(LRU_WIDTH)],
        out_shape=[jax.ShapeDtypeStruct((B, N_HEADS, S, HEAD_PAD), _BF16),
                   jax.ShapeDtypeStruct((B, N_HEADS, S, HEAD_PAD), _BF16),
                   jax.ShapeDtypeStruct((B, S, ATTN_WIDTH), _BF16),
                   jax.ShapeDtypeStruct((B, S, LRU_WIDTH), _F32),
                   jax.ShapeDtypeStruct((B, S, LRU_WIDTH), _F32)],
        compiler_params=_params(("parallel", "parallel")),
        name="in_proj",
    )(x, cos, sin, g_in, w_in_ext, g_qa, w_uq_ext, g_kva, w_uk, w_uv, gq, gk)


def _attention_kernel(q_ref, k_ref, v_ref, o_ref, m_ref, l_ref, acc_ref, *, tk):
    S = k_ref.shape[2]
    q = q_ref[0, 0]
    m_ref[...] = jnp.full(m_ref.shape, -jnp.inf, _F32)
    l_ref[...] = jnp.zeros(l_ref.shape, _F32)
    acc_ref[...] = jnp.zeros(acc_ref.shape, _F32)

    @pl.loop(0, S // tk)
    def _(j):
        rows = pl.ds(pl.multiple_of(j * tk, tk), tk)
        s = lax.dot_general(q, k_ref[0, 0, rows, :], (((1,), (1,)), ((), ())),
                            preferred_element_type=_F32)
        m_old = m_ref[...]
        m_new = jnp.maximum(m_old, jnp.max(s, axis=-1, keepdims=True))
        p = jnp.exp(s - m_new)
        alpha = jnp.exp(m_old - m_new)
        l_ref[...] = alpha * l_ref[...] + jnp.sum(p, axis=-1, keepdims=True)
        acc_ref[...] = alpha * acc_ref[...] + jnp.dot(
            p.astype(_BF16), v_ref[0, rows, :], preferred_element_type=_F32)
        m_ref[...] = m_new

    o_ref[0] = (acc_ref[...] / l_ref[...]).astype(o_ref.dtype)


def _attention(q, k, v, tq, tk):
    B, H, S, _ = q.shape
    return pl.pallas_call(
        functools.partial(_attention_kernel, tk=tk),
        grid=(B, H, S // tq),
        in_specs=[pl.BlockSpec((1, 1, tq, HEAD_PAD), lambda b, h, i: (b, h, i, 0)),
                  pl.BlockSpec((1, 1, S, HEAD_PAD), lambda b, h, i: (b, h, 0, 0)),
                  pl.BlockSpec((1, S, V_DIM), lambda b, h, i: (b, 0, h))],
        out_specs=pl.BlockSpec((1, tq, V_DIM), lambda b, h, i: (b, i, h)),
        out_shape=jax.ShapeDtypeStruct((B, S, ATTN_WIDTH), _BF16),
        scratch_shapes=[pltpu.VMEM((tq, 1), _F32), pltpu.VMEM((tq, 1), _F32),
                        pltpu.VMEM((tq, V_DIM), _F32)],
        compiler_params=_params(("parallel", "parallel", "arbitrary")),
        name="attention",
    )(q, k, v)


def _lru_gates(ux_ref, prev_ref, next_ref, cw_ref, cb_ref, wg_ref, bg_ref, lam_ref,
               xc_scr, a_scr, b_scr, *, tt, first, last):
    x = ux_ref[0]
    prev = jnp.where(first, 0.0, prev_ref[0])
    nxt = jnp.where(last, 0.0, next_ref[0])
    xe = jnp.concatenate([prev[SUBLANES - 1:], x, nxt[:2]], axis=0)
    cw = cw_ref[...]
    xc = cb_ref[...][None]
    for tap in range(CONV_W):
        xc = xc + xe[tap:tap + tt] * cw[tap][None]
    xc_scr[...] = xc.reshape(tt * LRU_BLOCKS, LRU_BD)
    for n in range(LRU_BLOCKS):
        rows = pl.ds(n, tt, stride=LRU_BLOCKS)
        xn = xc_scr[rows, :]
        z = jnp.dot(xn.astype(_BF16), wg_ref[n], preferred_element_type=_F32) + bg_ref[n]
        r = jax.nn.sigmoid(z[:, :LRU_BD])
        i = jax.nn.sigmoid(z[:, LRU_BD:])
        log_a = (-LRU_C) * r * jax.nn.softplus(-lam_ref[n])
        a = jnp.exp(log_a)
        a_scr[rows, :] = a
        b_scr[rows, :] = jnp.sqrt(jnp.tanh(-log_a) * (1.0 + a * a)) * (i * xn)


def _lru_fwd_kernel(ux_ref, prev_ref, next_ref, cw_ref, cb_ref, wg_ref, bg_ref, lam_ref,
                    hf_ref, xc_scr, a_scr, b_scr, h_scr, *, tt):
    i = pl.program_id(1)
    _lru_gates(ux_ref, prev_ref, next_ref, cw_ref, cb_ref, wg_ref, bg_ref, lam_ref,
               xc_scr, a_scr, b_scr, tt=tt, first=i == 0, last=i == pl.num_programs(1) - 1)

    @pl.when(i == 0)
    def _():
        h_scr[...] = jnp.zeros(h_scr.shape, _F32)

    def step(t, h):
        rows = pl.ds(pl.multiple_of(t * LRU_BLOCKS, LRU_BLOCKS), LRU_BLOCKS)
        h = a_scr[rows, :] * h + b_scr[rows, :]
        hf_ref[0, t] = h
        return h

    h_scr[...] = lax.fori_loop(0, tt, step, h_scr[...], unroll=8)


def _lru_bwd_kernel(ux_ref, prev_ref, next_ref, cw_ref, cb_ref, wg_ref, bg_ref, lam_ref,
                    hf_ref, ug_ref, out_ref, xc_scr, a_scr, b_scr, h_scr, *, tt):
    i = pl.program_id(1)
    _lru_gates(ux_ref, prev_ref, next_ref, cw_ref, cb_ref, wg_ref, bg_ref, lam_ref,
               xc_scr, a_scr, b_scr, tt=tt, first=i == pl.num_programs(1) - 1, last=i == 0)

    @pl.when(i == 0)
    def _():
        h_scr[...] = jnp.zeros(h_scr.shape, _F32)

    def step(k, h):
        t = tt - 1 - k
        rows = pl.ds(pl.multiple_of(t * LRU_BLOCKS, LRU_BLOCKS), LRU_BLOCKS)
        h = a_scr[rows, :] * h + b_scr[rows, :]
        out_ref[0, t] = (hf_ref[0, t] + h) * jax.nn.gelu(ug_ref[0, t])
        return h

    h_scr[...] = lax.fori_loop(0, tt, step, h_scr[...], unroll=8)


def _lru(ux, ug, conv_w, conv_b, wg, bg, lam, tt):
    B, S = ux.shape[:2]
    nt = S // tt
    nh = tt // SUBLANES
    tile4 = (1, tt, LRU_BLOCKS, LRU_BD)
    halo4 = (1, SUBLANES, LRU_BLOCKS, LRU_BD)
    last_halo = S // SUBLANES - 1
    scratch = [pltpu.VMEM((tt * LRU_BLOCKS, LRU_BD), _F32) for _ in range(3)]
    scratch.append(pltpu.VMEM((LRU_BLOCKS, LRU_BD), _F32))
    consts = [_const_spec((CONV_W, LRU_BLOCKS, LRU_BD)), _const_spec((LRU_BLOCKS, LRU_BD)),
              _const_spec((LRU_BLOCKS, LRU_BD, 2 * LRU_BD)),
              _const_spec((LRU_BLOCKS, 1, 2 * LRU_BD)), _const_spec((LRU_BLOCKS, 1, LRU_BD))]

    def specs(tile_of):
        return [pl.BlockSpec(tile4, lambda b, i: (b, tile_of(i), 0, 0)),
                pl.BlockSpec(halo4, lambda b, i: (b, jnp.maximum(tile_of(i) * nh - 1, 0), 0, 0)),
                pl.BlockSpec(halo4, lambda b, i: (b, jnp.minimum((tile_of(i) + 1) * nh, last_halo), 0, 0))]

    fwd_tile = lambda i: i
    bwd_tile = lambda i: nt - 1 - i
    shape4 = jax.ShapeDtypeStruct((B, S, LRU_BLOCKS, LRU_BD), _F32)
    hf = pl.pallas_call(
        functools.partial(_lru_fwd_kernel, tt=tt),
        grid=(B, nt),
        in_specs=specs(fwd_tile) + consts,
        out_specs=pl.BlockSpec(tile4, lambda b, i: (b, i, 0, 0)),
        out_shape=shape4,
        scratch_shapes=scratch,
        compiler_params=_params(("parallel", "arbitrary")),
        name="lru_forward",
    )(ux, ux, ux, conv_w, conv_b, wg[0], bg[0], lam[0])
    bwd_spec = pl.BlockSpec(tile4, lambda b, i: (b, nt - 1 - i, 0, 0))
    return pl.pallas_call(
        functools.partial(_lru_bwd_kernel, tt=tt),
        grid=(B, nt),
        in_specs=specs(bwd_tile) + consts + [bwd_spec, bwd_spec],
        out_specs=bwd_spec,
        out_shape=shape4,
        scratch_shapes=scratch,
        compiler_params=_params(("parallel", "arbitrary")),
        name="lru_backward",
    )(ux, ux, ux, conv_w, conv_b, wg[1], bg[1], lam[1], hf, ug)


def _out_proj_kernel(x_ref, attn_ref, lru_ref, ga_ref, gl_ref, wa_ref, wl_ref, y_ref):
    a = _rms(attn_ref[0].astype(_F32), ga_ref[...]).astype(_BF16)
    l = _rms(lru_ref[0], gl_ref[...]).astype(_BF16)
    y = jnp.dot(a, wa_ref[...], preferred_element_type=_F32)
    y = y + jnp.dot(l, wl_ref[...], preferred_element_type=_F32)
    y_ref[0] = x_ref[0] + y


def _out_proj(x, attn, lru, ga, gl, w_out_a, w_out_l, tm):
    B, S, _ = x.shape
    tok = lambda w: pl.BlockSpec((1, tm, w), lambda b, i: (b, i, 0))
    return pl.pallas_call(
        _out_proj_kernel,
        grid=(B, S // tm),
        in_specs=[tok(D_MODEL), tok(ATTN_WIDTH), tok(LRU_WIDTH),
                  _const_spec((1, ATTN_WIDTH)), _const_spec((1, LRU_WIDTH)),
                  _const_spec((ATTN_WIDTH, D_MODEL)), _const_spec((LRU_WIDTH, D_MODEL))],
        out_specs=tok(D_MODEL),
        out_shape=jax.ShapeDtypeStruct(x.shape, _F32),
        compiler_params=_params(("parallel", "parallel")),
        name="out_proj",
    )(x, attn, lru, ga, gl, w_out_a, w_out_l)


def _ffn_kernel(x_ref, g_ref, wg_ref, wu_ref, wd_ref, y_ref, h_scr, acc_scr):
    j = pl.program_id(2)

    @pl.when(j == 0)
    def _():
        h_scr[...] = _rms(x_ref[0], g_ref[...]).astype(_BF16)

    h = h_scr[...]
    gate = jnp.dot(h, wg_ref[...], preferred_element_type=_F32)
    up = jnp.dot(h, wu_ref[...], preferred_element_type=_F32)
    act = (jax.nn.silu(gate) * up).astype(_BF16)
    part = jnp.dot(act, wd_ref[...], preferred_element_type=_F32)

    @pl.when(j == 0)
    def _():
        acc_scr[...] = part

    @pl.when(j > 0)
    def _():
        acc_scr[...] += part

    @pl.when(j == pl.num_programs(2) - 1)
    def _():
        y_ref[0] = x_ref[0] + acc_scr[...]


def _ffn(x, g, w_gate, w_up, w_down, tm, tf):
    B, S, _ = x.shape
    tok = pl.BlockSpec((1, tm, D_MODEL), lambda b, i, j: (b, i, 0))
    return pl.pallas_call(
        _ffn_kernel,
        grid=(B, S // tm, D_FF // tf),
        in_specs=[tok, pl.BlockSpec((1, D_MODEL), lambda b, i, j: (0, 0)),
                  pl.BlockSpec((D_MODEL, tf), lambda b, i, j: (0, j)),
                  pl.BlockSpec((D_MODEL, tf), lambda b, i, j: (0, j)),
                  pl.BlockSpec((tf, D_MODEL), lambda b, i, j: (j, 0))],
        out_specs=tok,
        out_shape=jax.ShapeDtypeStruct(x.shape, _F32),
        scratch_shapes=[pltpu.VMEM((tm, D_MODEL), _BF16), pltpu.VMEM((tm, D_MODEL), _F32)],
        compiler_params=_params(("parallel", "parallel", "arbitrary")),
        name="ffn",
    )(x, g, w_gate, w_up, w_down)


def _swap_halves(w):
    half = ROPE // 2
    return jnp.concatenate([-w[..., half:], w[..., :half]], axis=-1)


def _swap_gain(g):
    half = ROPE // 2
    return jnp.concatenate([g[half:], g[:half]])


def _gain_rows(g):
    pad = jnp.zeros((LANES - ROPE,), _F32)
    pe = g[NOPE:]
    return jnp.stack([g[:NOPE], jnp.concatenate([pe, pad]),
                      jnp.concatenate([_swap_gain(pe), pad])])


def _prepare(attn_norm_g, w_in, q_a_norm_g, w_uq, kv_a_norm_g, w_ukv, q_norm_g, k_norm_g,
             conv_w, conv_b, w_rg_r, b_rg_r, w_rg_i, b_rg_i, lru_lambda,
             attn_out_norm_g, lru_out_norm_g, w_out, ffn_norm_g, w_gate, w_up, w_down):
    k_pe = w_in[:, OFF_KV:OFF_KV + ROPE]
    w_in_ext = jnp.concatenate(
        [w_in[:, :OFF_KV + ROPE], _swap_halves(k_pe), w_in[:, OFF_KV + ROPE:]], axis=1).astype(_BF16)
    wq = w_uq.reshape(Q_RANK, N_HEADS, QK_DIM)
    w_uq_ext = jnp.concatenate([wq, _swap_halves(wq[..., NOPE:])], axis=-1)
    w_uq_ext = w_uq_ext.reshape(Q_RANK, N_HEADS * HEAD_PAD).astype(_BF16)
    wkv = w_ukv.reshape(KV_RANK, N_HEADS, NOPE + V_DIM)
    w_uk = wkv[..., :NOPE].reshape(KV_RANK, N_HEADS * NOPE).astype(_BF16)
    w_uv = wkv[..., NOPE:].reshape(KV_RANK, ATTN_WIDTH).astype(_BF16)
    wg = jnp.concatenate([w_rg_r, w_rg_i], axis=-1).astype(_BF16)
    bg = jnp.concatenate([b_rg_r, b_rg_i], axis=-1)[:, :, None, :]
    lam = lru_lambda.reshape(2, LRU_BLOCKS, 1, LRU_BD)
    return dict(
        g_in=attn_norm_g[None], w_in_ext=w_in_ext, g_qa=q_a_norm_g[None], w_uq_ext=w_uq_ext,
        g_kva=kv_a_norm_g[None], w_uk=w_uk, w_uv=w_uv,
        gq=_gain_rows(q_norm_g), gk=_gain_rows(k_norm_g),
        conv_w=conv_w.reshape(CONV_W, LRU_BLOCKS, LRU_BD), conv_b=conv_b.reshape(LRU_BLOCKS, LRU_BD),
        wg=wg, bg=bg, lam=lam,
        ga=attn_out_norm_g[None], gl=lru_out_norm_g[None],
        w_out_a=w_out[:ATTN_WIDTH].astype(_BF16), w_out_l=w_out[ATTN_WIDTH:].astype(_BF16),
        g_ffn=ffn_norm_g[None], w_gate=w_gate.astype(_BF16), w_up=w_up.astype(_BF16),
        w_down=w_down.astype(_BF16))


def _rope_tables(S):
    half = ROPE // 2
    inv_freq = ROPE_BASE ** (-2.0 * jnp.arange(half, dtype=_F32) / ROPE)
    ang = jnp.arange(S, dtype=_F32)[:, None] * inv_freq[None, :]
    pad = jnp.zeros((S, LANES - ROPE), _F32)
    cos = jnp.concatenate([jnp.cos(ang), jnp.cos(ang), pad], axis=1)
    sin = jnp.concatenate([jnp.sin(ang), jnp.sin(ang), pad], axis=1)
    return cos, sin


def _tiles(S):
    return dict(in_proj=min(256, S), attn_q=min(512, S), attn_k=min(512, S),
                lru=min(256, S), out_proj=min(512, S), ffn=min(512, S), ffn_cols=512)


def _encoder_layer(x, p):
    B, S, _ = x.shape
    t = _tiles(S)
    cos, sin = _rope_tables(S)
    q, k, v, ux, ug = _in_proj(x, cos, sin, p["g_in"], p["w_in_ext"], p["g_qa"], p["w_uq_ext"],
                               p["g_kva"], p["w_uk"], p["w_uv"], p["gq"], p["gk"], t["in_proj"])
    attn = _attention(q, k, v, t["attn_q"], t["attn_k"])
    shape4 = (B, S, LRU_BLOCKS, LRU_BD)
    lru = _lru(ux.reshape(shape4), ug.reshape(shape4), p["conv_w"], p["conv_b"],
               p["wg"], p["bg"], p["lam"], t["lru"]).reshape(B, S, LRU_WIDTH)
    x1 = _out_proj(x, attn, lru, p["ga"], p["gl"], p["w_out_a"], p["w_out_l"], t["out_proj"])
    return _ffn(x1, p["g_ffn"], p["w_gate"], p["w_up"], p["w_down"], t["ffn"], t["ffn_cols"])


def kernel(x_prompt, x_sample, attn_norm_g, w_in, q_a_norm_g, w_uq, kv_a_norm_g, w_ukv, q_norm_g, k_norm_g, conv_w, conv_b, w_rg_r, b_rg_r, w_rg_i, b_rg_i, lru_lambda, attn_out_norm_g, lru_out_norm_g, w_out, ffn_norm_g, w_gate, w_up, w_down):
    depth = attn_norm_g.shape[0]
    weights = (attn_norm_g, w_in, q_a_norm_g, w_uq, kv_a_norm_g, w_ukv, q_norm_g, k_norm_g,
               conv_w, conv_b, w_rg_r, b_rg_r, w_rg_i, b_rg_i, lru_lambda,
               attn_out_norm_g, lru_out_norm_g, w_out, ffn_norm_g, w_gate, w_up, w_down)
    y_prompt, y_sample = x_prompt, x_sample
    for layer in range(depth):
        p = _prepare(*(w[layer] for w in weights))
        y_prompt = _encoder_layer(y_prompt, p)
        y_sample = _encoder_layer(y_sample, p)
    return (y_prompt, y_sample)
```

```python
import functools

import jax
import jax.numpy as jnp
from jax import lax
from jax.experimental import pallas as pl
from jax.experimental.pallas import tpu as pltpu

D_MODEL = 2048
N_HEADS = 8
NOPE = 128
ROPE = 64
QK_DIM = NOPE + ROPE
V_DIM = 128
V_EXT = V_DIM + 16
ATTN_WIDTH = N_HEADS * V_DIM
Q_RANK = 512
KV_RANK = 256
ROPE_BASE = 10000.0
LRU_WIDTH = D_MODEL - ATTN_WIDTH
LRU_BLOCKS = 8
LRU_BD = LRU_WIDTH // LRU_BLOCKS
CONV_W = 4
LRU_C = 8.0
D_FF = 5632
EPS = 1e-6
LOG2_E = 1.4426950408889634

LANES = 128
SUBLANES = 8
HEAD_PAD = 2 * LANES
OFF_Q = Q_RANK
OFF_KV = OFF_Q + KV_RANK
OFF_KG = OFF_KV + LANES
OFF_UX = OFF_KG + LRU_WIDTH
IN_COLS_EXT = OFF_UX + LRU_WIDTH
VMEM_LIMIT = 56 * 1024 * 1024

_F32 = jnp.float32
_BF16 = jnp.bfloat16


def _const_spec(shape):
    return pl.BlockSpec(shape, lambda *_: (0,) * len(shape), pipeline_mode=pl.Buffered(1))


def _params(sem):
    return pltpu.CompilerParams(dimension_semantics=sem, vmem_limit_bytes=VMEM_LIMIT)


def _rms(x, g):
    return x * lax.rsqrt(jnp.mean(x * x, axis=-1, keepdims=True) + EPS) * g


def _in_proj_kernel(x_ref, cos_ref, sin_ref, g_in_ref, w_in_ref, g_qa_ref, w_uq_ref,
                    g_kva_ref, w_uk_ref, w_uvt_ref, gq_ref, gk_ref,
                    q_ref, k_ref, vt_ref, ux_ref, ug_ref):
    x = x_ref[0]
    h = _rms(x, g_in_ref[...]).astype(_BF16)
    u = jnp.dot(h, w_in_ref[...], preferred_element_type=_F32)
    ux_ref[0] = u[:, OFF_KG:OFF_UX]
    ug_ref[0] = u[:, OFF_UX:]

    cos = cos_ref[...]
    sin = sin_ref[...]
    lane = lax.broadcasted_iota(jnp.int32, (1, LANES), 1)
    rope_mask = (lane < ROPE).astype(_F32)
    scale = QK_DIM ** -0.5 * LOG2_E

    def rotary(grp, g_rows):
        return grp * (g_rows[1:2] * cos) + pltpu.roll(grp, ROPE, axis=1) * (g_rows[2:3] * sin)

    def head_rms(nope, grp, extra=None):
        ss = jnp.sum(nope * nope, axis=-1, keepdims=True)
        ss = ss + (jnp.sum(grp * grp * rope_mask, axis=-1, keepdims=True) if extra is None else extra)
        return lax.rsqrt(ss * (1.0 / QK_DIM) + EPS)

    cq = _rms(u[:, :OFF_Q], g_qa_ref[...]).astype(_BF16)
    qf = jnp.dot(cq, w_uq_ref[...], preferred_element_type=_F32)
    gq = gq_ref[...]
    for hd in range(N_HEADS):
        nope = qf[:, hd * HEAD_PAD:hd * HEAD_PAD + NOPE]
        grp = qf[:, hd * HEAD_PAD + NOPE:(hd + 1) * HEAD_PAD]
        r = head_rms(nope, grp) * scale
        q_ref[0, hd, :, :NOPE] = (nope * r * gq[0:1]).astype(_BF16)
        q_ref[0, hd, :, NOPE:] = (rotary(grp, gq) * r).astype(_BF16)

    ckv = _rms(u[:, OFF_Q:OFF_KV], g_kva_ref[...]).astype(_BF16)
    vt = lax.dot_general(w_uvt_ref[...], ckv, (((1,), (1,)), ((), ())), preferred_element_type=_F32)
    ones_rows = (lax.broadcasted_iota(jnp.int32, (V_EXT - V_DIM, vt.shape[1]), 0) == 0).astype(_BF16)
    for hd in range(N_HEADS):
        vt_ref[0, hd, :V_DIM, :] = vt[hd * V_DIM:(hd + 1) * V_DIM].astype(_BF16)
        vt_ref[0, hd, V_DIM:, :] = ones_rows
    kn = jnp.dot(ckv, w_uk_ref[...], preferred_element_type=_F32)
    gk = gk_ref[...]
    kgrp = u[:, OFF_KV:OFF_KG]
    ss_pe = jnp.sum(kgrp * kgrp * rope_mask, axis=-1, keepdims=True)
    krot = rotary(kgrp, gk)
    for hd in range(N_HEADS):
        nope = kn[:, hd * NOPE:(hd + 1) * NOPE]
        r = head_rms(nope, None, ss_pe)
        k_ref[0, hd, :, :NOPE] = (nope * r * gk[0:1]).astype(_BF16)
        k_ref[0, hd, :, NOPE:] = (krot * r).astype(_BF16)


def _in_proj(x, cos, sin, g_in, w_in_ext, g_qa, w_uq_ext, g_kva, w_uk, w_uvt, gq, gk, tm):
    B, S, _ = x.shape
    grid = (B, S // tm)
    tok = lambda w: pl.BlockSpec((1, tm, w), lambda b, i: (b, i, 0))
    head = pl.BlockSpec((1, N_HEADS, tm, HEAD_PAD), lambda b, i: (b, 0, i, 0))
    tab = pl.BlockSpec((tm, LANES), lambda b, i: (i, 0))
    return pl.pallas_call(
        _in_proj_kernel,
        grid=grid,
        in_specs=[tok(D_MODEL), tab, tab,
                  _const_spec((1, D_MODEL)), _const_spec((D_MODEL, IN_COLS_EXT)),
                  _const_spec((1, Q_RANK)), _const_spec((Q_RANK, N_HEADS * HEAD_PAD)),
                  _const_spec((1, KV_RANK)), _const_spec((KV_RANK, N_HEADS * NOPE)),
                  _const_spec((ATTN_WIDTH, KV_RANK)),
                  _const_spec((3, LANES)), _const_spec((3, LANES))],
        out_specs=[head, head, pl.BlockSpec((1, N_HEADS, V_EXT, tm), lambda b, i: (b, 0, 0, i)),
                   tok(LRU_WIDTH), tok(LRU_WIDTH)],
        out_shape=[jax.ShapeDtypeStruct((B, N_HEADS, S, HEAD_PAD), _BF16),
                   jax.ShapeDtypeStruct((B, N_HEADS, S, HEAD_PAD), _BF16),
                   jax.ShapeDtypeStruct((B, N_HEADS, V_EXT, S), _BF16),
                   jax.ShapeDtypeStruct((B, S, LRU_WIDTH), _F32),
                   jax.ShapeDtypeStruct((B, S, LRU_WIDTH), _F32)],
        compiler_params=_params(("parallel", "parallel")),
        name="in_proj",
    )(x, cos, sin, g_in, w_in_ext, g_qa, w_uq_ext, g_kva, w_uk, w_uvt, gq, gk)


def _attention_kernel(q_ref, k_ref, vt_ref, o_ref, s_scr, *, tk, unroll):
    n = k_ref.shape[2] // tk
    q = q_ref[0, 0]
    tq = q.shape[0]

    def scores(j, slot):
        rows = pl.ds(pl.multiple_of(j * tk, tk), tk)
        s_scr[slot] = lax.dot_general(k_ref[0, 0, rows, :], q, (((1,), (1,)), ((), ())),
                                      preferred_element_type=_F32)

    def update(j, slot, m, acc):
        cols = pl.ds(pl.multiple_of(j * tk, tk), tk)
        m_new = jnp.maximum(m, jnp.max(s_scr[slot], axis=0, keepdims=True))
        p = jnp.exp2(s_scr[slot] - m_new).astype(_BF16)
        acc = jnp.exp2(m - m_new) * acc + jnp.dot(vt_ref[0, 0, :, cols], p,
                                                  preferred_element_type=_F32)
        return m_new, acc

    def body(jj, carry):
        m, acc = carry
        j = 2 * jj
        scores(j + 1, 1)
        m, acc = update(j, 0, m, acc)
        scores(j + 2, 0)
        return update(j + 1, 1, m, acc)

    scores(0, 0)
    carry = (jnp.full((1, tq), -jnp.inf, _F32), jnp.zeros((V_EXT, tq), _F32))
    m, acc = lax.fori_loop(0, n // 2 - 1, body, carry, unroll=unroll)
    scores(n - 1, 1)
    m, acc = update(n - 2, 0, m, acc)
    m, acc = update(n - 1, 1, m, acc)
    out = acc[:V_DIM] / acc[V_DIM:V_DIM + 1]
    o_ref[0] = out.T.astype(o_ref.dtype)


def _attention(q, k, vt, tq, tk):
    B, H, S, _ = q.shape
    pairs = S // tk // 2 - 1
    assert S % (2 * tk) == 0 and pairs >= 0
    unroll = max(u for u in range(1, 6) if pairs % u == 0) if pairs else 1
    return pl.pallas_call(
        functools.partial(_attention_kernel, tk=tk, unroll=unroll),
        grid=(B, H, S // tq),
        in_specs=[pl.BlockSpec((1, 1, tq, HEAD_PAD), lambda b, h, i: (b, h, i, 0)),
                  pl.BlockSpec((1, 1, S, HEAD_PAD), lambda b, h, i: (b, h, 0, 0)),
                  pl.BlockSpec((1, 1, V_EXT, S), lambda b, h, i: (b, h, 0, 0))],
        out_specs=pl.BlockSpec((1, tq, V_DIM), lambda b, h, i: (b, i, h)),
        out_shape=jax.ShapeDtypeStruct((B, S, ATTN_WIDTH), _BF16),
        scratch_shapes=[pltpu.VMEM((2, tk, tq), _F32)],
        compiler_params=_params(("parallel", "parallel", "arbitrary")),
        name="attention",
    )(q, k, vt)


def _lru_gates(ux_ref, prev_ref, next_ref, cw_ref, cb_ref, wg_ref, bg_ref, lam_ref,
               xc_scr, a_scr, b_scr, *, tt, first, last):
    x = ux_ref[0]
    prev = jnp.where(first, 0.0, prev_ref[0])
    nxt = jnp.where(last, 0.0, next_ref[0])
    xe = jnp.concatenate([prev[SUBLANES - 1:], x, nxt[:2]], axis=0)
    cw = cw_ref[...]
    xc = cb_ref[...][None]
    for tap in range(CONV_W):
        xc = xc + xe[tap:tap + tt] * cw[tap][None]
    xc_scr[...] = xc.reshape(tt * LRU_BLOCKS, LRU_BD)
    for n in range(LRU_BLOCKS):
        rows = pl.ds(n, tt, stride=LRU_BLOCKS)
        xn = xc_scr[rows, :]
        z = jnp.dot(xn.astype(_BF16), wg_ref[n], preferred_element_type=_F32) + bg_ref[n]
        r = jax.nn.sigmoid(z[:, :LRU_BD])
        i = jax.nn.sigmoid(z[:, LRU_BD:])
        log_a = (-LRU_C) * r * jax.nn.softplus(-lam_ref[n])
        a = jnp.exp(log_a)
        a_scr[rows, :] = a
        b_scr[rows, :] = jnp.sqrt(jnp.tanh(-log_a) * (1.0 + a * a)) * (i * xn)


def _lru_fwd_kernel(ux_ref, prev_ref, next_ref, cw_ref, cb_ref, wg_ref, bg_ref, lam_ref,
                    hf_ref, xc_scr, a_scr, b_scr, h_scr, *, tt):
    i = pl.program_id(1)
    _lru_gates(ux_ref, prev_ref, next_ref, cw_ref, cb_ref, wg_ref, bg_ref, lam_ref,
               xc_scr, a_scr, b_scr, tt=tt, first=i == 0, last=i == pl.num_programs(1) - 1)

    @pl.when(i == 0)
    def _():
        h_scr[...] = jnp.zeros(h_scr.shape, _F32)

    def step(t, h):
        rows = pl.ds(pl.multiple_of(t * LRU_BLOCKS, LRU_BLOCKS), LRU_BLOCKS)
        h = a_scr[rows, :] * h + b_scr[rows, :]
        hf_ref[0, t] = h
        return h

    h_scr[...] = lax.fori_loop(0, tt, step, h_scr[...], unroll=8)


def _lru_bwd_kernel(ux_ref, prev_ref, next_ref, cw_ref, cb_ref, wg_ref, bg_ref, lam_ref,
                    hf_ref, ug_ref, out_ref, xc_scr, a_scr, b_scr, h_scr, *, tt):
    i = pl.program_id(1)
    _lru_gates(ux_ref, prev_ref, next_ref, cw_ref, cb_ref, wg_ref, bg_ref, lam_ref,
               xc_scr, a_scr, b_scr, tt=tt, first=i == pl.num_programs(1) - 1, last=i == 0)

    @pl.when(i == 0)
    def _():
        h_scr[...] = jnp.zeros(h_scr.shape, _F32)

    def step(k, h):
        t = tt - 1 - k
        rows = pl.ds(pl.multiple_of(t * LRU_BLOCKS, LRU_BLOCKS), LRU_BLOCKS)
        h = a_scr[rows, :] * h + b_scr[rows, :]
        out_ref[0, t] = (hf_ref[0, t] + h) * jax.nn.gelu(ug_ref[0, t])
        return h

    h_scr[...] = lax.fori_loop(0, tt, step, h_scr[...], unroll=8)


def _lru(ux, ug, conv_w, conv_b, wg, bg, lam, tt):
    B, S = ux.shape[:2]
    nt = S // tt
    nh = tt // SUBLANES
    tile4 = (1, tt, LRU_BLOCKS, LRU_BD)
    halo4 = (1, SUBLANES, LRU_BLOCKS, LRU_BD)
    last_halo = S // SUBLANES - 1
    scratch = [pltpu.VMEM((tt * LRU_BLOCKS, LRU_BD), _F32) for _ in range(3)]
    scratch.append(pltpu.VMEM((LRU_BLOCKS, LRU_BD), _F32))
    consts = [_const_spec((CONV_W, LRU_BLOCKS, LRU_BD)), _const_spec((LRU_BLOCKS, LRU_BD)),
              _const_spec((LRU_BLOCKS, LRU_BD, 2 * LRU_BD)),
              _const_spec((LRU_BLOCKS, 1, 2 * LRU_BD)), _const_spec((LRU_BLOCKS, 1, LRU_BD))]

    def specs(tile_of):
        return [pl.BlockSpec(tile4, lambda b, i: (b, tile_of(i), 0, 0)),
                pl.BlockSpec(halo4, lambda b, i: (b, jnp.maximum(tile_of(i) * nh - 1, 0), 0, 0)),
                pl.BlockSpec(halo4, lambda b, i: (b, jnp.minimum((tile_of(i) + 1) * nh, last_halo), 0, 0))]

    fwd_tile = lambda i: i
    bwd_tile = lambda i: nt - 1 - i
    shape4 = jax.ShapeDtypeStruct((B, S, LRU_BLOCKS, LRU_BD), _F32)
    hf = pl.pallas_call(
        functools.partial(_lru_fwd_kernel, tt=tt),
        grid=(B, nt),
        in_specs=specs(fwd_tile) + consts,
        out_specs=pl.BlockSpec(tile4, lambda b, i: (b, i, 0, 0)),
        out_shape=shape4,
        scratch_shapes=scratch,
        compiler_params=_params(("parallel", "arbitrary")),
        name="lru_forward",
    )(ux, ux, ux, conv_w, conv_b, wg[0], bg[0], lam[0])
    bwd_spec = pl.BlockSpec(tile4, lambda b, i: (b, nt - 1 - i, 0, 0))
    return pl.pallas_call(
        functools.partial(_lru_bwd_kernel, tt=tt),
        grid=(B, nt),
        in_specs=specs(bwd_tile) + consts + [bwd_spec, bwd_spec],
        out_specs=bwd_spec,
        out_shape=shape4,
        scratch_shapes=scratch,
        compiler_params=_params(("parallel", "arbitrary")),
        name="lru_backward",
    )(ux, ux, ux, conv_w, conv_b, wg[1], bg[1], lam[1], hf, ug)


def _out_proj_kernel(x_ref, attn_ref, lru_ref, ga_ref, gl_ref, wa_ref, wl_ref, y_ref):
    a = _rms(attn_ref[0].astype(_F32), ga_ref[...]).astype(_BF16)
    l = _rms(lru_ref[0], gl_ref[...]).astype(_BF16)
    y = jnp.dot(a, wa_ref[...], preferred_element_type=_F32)
    y = y + jnp.dot(l, wl_ref[...], preferred_element_type=_F32)
    y_ref[0] = x_ref[0] + y


def _out_proj(x, attn, lru, ga, gl, w_out_a, w_out_l, tm):
    B, S, _ = x.shape
    tok = lambda w: pl.BlockSpec((1, tm, w), lambda b, i: (b, i, 0))
    return pl.pallas_call(
        _out_proj_kernel,
        grid=(B, S // tm),
        in_specs=[tok(D_MODEL), tok(ATTN_WIDTH), tok(LRU_WIDTH),
                  _const_spec((1, ATTN_WIDTH)), _const_spec((1, LRU_WIDTH)),
                  _const_spec((ATTN_WIDTH, D_MODEL)), _const_spec((LRU_WIDTH, D_MODEL))],
        out_specs=tok(D_MODEL),
        out_shape=jax.ShapeDtypeStruct(x.shape, _F32),
        compiler_params=_params(("parallel", "parallel")),
        name="out_proj",
    )(x, attn, lru, ga, gl, w_out_a, w_out_l)


def _ffn_kernel(x_ref, g_ref, wg_ref, wu_ref, wd_ref, y_ref, h_scr, acc_scr):
    j = pl.program_id(2)

    @pl.when(j == 0)
    def _():
        h_scr[...] = _rms(x_ref[0], g_ref[...]).astype(_BF16)

    h = h_scr[...]
    gate = jnp.dot(h, wg_ref[...], preferred_element_type=_F32)
    up = jnp.dot(h, wu_ref[...], preferred_element_type=_F32)
    act = (jax.nn.silu(gate) * up).astype(_BF16)
    part = jnp.dot(act, wd_ref[...], preferred_element_type=_F32)

    @pl.when(j == 0)
    def _():
        acc_scr[...] = part

    @pl.when(j > 0)
    def _():
        acc_scr[...] += part

    @pl.when(j == pl.num_programs(2) - 1)
    def _():
        y_ref[0] = x_ref[0] + acc_scr[...]


def _ffn(x, g, w_gate, w_up, w_down, tm, tf):
    B, S, _ = x.shape
    tok = pl.BlockSpec((1, tm, D_MODEL), lambda b, i, j: (b, i, 0))
    return pl.pallas_call(
        _ffn_kernel,
        grid=(B, S // tm, D_FF // tf),
        in_specs=[tok, pl.BlockSpec((1, D_MODEL), lambda b, i, j: (0, 0)),
                  pl.BlockSpec((D_MODEL, tf), lambda b, i, j: (0, j)),
                  pl.BlockSpec((D_MODEL, tf), lambda b, i, j: (0, j)),
                  pl.BlockSpec((tf, D_MODEL), lambda b, i, j: (j, 0))],
        out_specs=tok,
        out_shape=jax.ShapeDtypeStruct(x.shape, _F32),
        scratch_shapes=[pltpu.VMEM((tm, D_MODEL), _BF16), pltpu.VMEM((tm, D_MODEL), _F32)],
        compiler_params=_params(("parallel", "parallel", "arbitrary")),
        name="ffn",
    )(x, g, w_gate, w_up, w_down)


def _swap_halves(w):
    half = ROPE // 2
    return jnp.concatenate([-w[..., half:], w[..., :half]], axis=-1)


def _swap_gain(g):
    half = ROPE // 2
    return jnp.concatenate([g[half:], g[:half]])


def _gain_rows(g):
    pad = jnp.zeros((LANES - ROPE,), _F32)
    pe = g[NOPE:]
    return jnp.stack([g[:NOPE], jnp.concatenate([pe, pad]),
                      jnp.concatenate([_swap_gain(pe), pad])])


def _prepare(attn_norm_g, w_in, q_a_norm_g, w_uq, kv_a_norm_g, w_ukv, q_norm_g, k_norm_g,
             conv_w, conv_b, w_rg_r, b_rg_r, w_rg_i, b_rg_i, lru_lambda,
             attn_out_norm_g, lru_out_norm_g, w_out, ffn_norm_g, w_gate, w_up, w_down):
    k_pe = w_in[:, OFF_KV:OFF_KV + ROPE]
    w_in_ext = jnp.concatenate(
        [w_in[:, :OFF_KV + ROPE], _swap_halves(k_pe), w_in[:, OFF_KV + ROPE:]], axis=1).astype(_BF16)
    wq = w_uq.reshape(Q_RANK, N_HEADS, QK_DIM)
    w_uq_ext = jnp.concatenate([wq, _swap_halves(wq[..., NOPE:])], axis=-1)
    w_uq_ext = w_uq_ext.reshape(Q_RANK, N_HEADS * HEAD_PAD).astype(_BF16)
    wkv = w_ukv.reshape(KV_RANK, N_HEADS, NOPE + V_DIM)
    w_uk = wkv[..., :NOPE].reshape(KV_RANK, N_HEADS * NOPE).astype(_BF16)
    w_uvt = wkv[..., NOPE:].reshape(KV_RANK, ATTN_WIDTH).T.astype(_BF16)
    wg = jnp.concatenate([w_rg_r, w_rg_i], axis=-1).astype(_BF16)
    bg = jnp.concatenate([b_rg_r, b_rg_i], axis=-1)[:, :, None, :]
    lam = lru_lambda.reshape(2, LRU_BLOCKS, 1, LRU_BD)
    return dict(
        g_in=attn_norm_g[None], w_in_ext=w_in_ext, g_qa=q_a_norm_g[None], w_uq_ext=w_uq_ext,
        g_kva=kv_a_norm_g[None], w_uk=w_uk, w_uvt=w_uvt,
        gq=_gain_rows(q_norm_g), gk=_gain_rows(k_norm_g),
        conv_w=conv_w.reshape(CONV_W, LRU_BLOCKS, LRU_BD), conv_b=conv_b.reshape(LRU_BLOCKS, LRU_BD),
        wg=wg, bg=bg, lam=lam,
        ga=attn_out_norm_g[None], gl=lru_out_norm_g[None],
        w_out_a=w_out[:ATTN_WIDTH].astype(_BF16), w_out_l=w_out[ATTN_WIDTH:].astype(_BF16),
        g_ffn=ffn_norm_g[None], w_gate=w_gate.astype(_BF16), w_up=w_up.astype(_BF16),
        w_down=w_down.astype(_BF16))


def _rope_tables(S):
    half = ROPE // 2
    inv_freq = ROPE_BASE ** (-2.0 * jnp.arange(half, dtype=_F32) / ROPE)
    ang = jnp.arange(S, dtype=_F32)[:, None] * inv_freq[None, :]
    pad = jnp.zeros((S, LANES - ROPE), _F32)
    cos = jnp.concatenate([jnp.cos(ang), jnp.cos(ang), pad], axis=1)
    sin = jnp.concatenate([jnp.sin(ang), jnp.sin(ang), pad], axis=1)
    return cos, sin


def _tiles(S):
    return dict(in_proj=min(256, S), attn_q=min(512, S), attn_k=min(512, S // 2),
                lru=min(256, S), out_proj=min(512, S), ffn=min(512, S), ffn_cols=512)


def _encoder_layer(x, p):
    B, S, _ = x.shape
    t = _tiles(S)
    cos, sin = _rope_tables(S)
    q, k, vt, ux, ug = _in_proj(x, cos, sin, p["g_in"], p["w_in_ext"], p["g_qa"], p["w_uq_ext"],
                               p["g_kva"], p["w_uk"], p["w_uvt"], p["gq"], p["gk"], t["in_proj"])
    attn = _attention(q, k, vt, t["attn_q"], t["attn_k"])
    shape4 = (B, S, LRU_BLOCKS, LRU_BD)
    lru = _lru(ux.reshape(shape4), ug.reshape(shape4), p["conv_w"], p["conv_b"],
               p["wg"], p["bg"], p["lam"], t["lru"]).reshape(B, S, LRU_WIDTH)
    x1 = _out_proj(x, attn, lru, p["ga"], p["gl"], p["w_out_a"], p["w_out_l"], t["out_proj"])
    return _ffn(x1, p["g_ffn"], p["w_gate"], p["w_up"], p["w_down"], t["ffn"], t["ffn_cols"])


def kernel(x_prompt, x_sample, attn_norm_g, w_in, q_a_norm_g, w_uq, kv_a_norm_g, w_ukv, q_norm_g, k_norm_g, conv_w, conv_b, w_rg_r, b_rg_r, w_rg_i, b_rg_i, lru_lambda, attn_out_norm_g, lru_out_norm_g, w_out, ffn_norm_g, w_gate, w_up, w_down):
    depth = attn_norm_g.shape[0]
    weights = (attn_norm_g, w_in, q_a_norm_g, w_uq, kv_a_norm_g, w_ukv, q_norm_g, k_norm_g,
               conv_w, conv_b, w_rg_r, b_rg_r, w_rg_i, b_rg_i, lru_lambda,
               attn_out_norm_g, lru_out_norm_g, w_out, ffn_norm_g, w_gate, w_up, w_down)
    y_prompt, y_sample = x_prompt, x_sample
    for layer in range(depth):
        p = _prepare(*(w[layer] for w in weights))
        y_prompt = _encoder_layer(y_prompt, p)
        y_sample = _encoder_layer(y_sample, p)
    return (y_prompt, y_sample)
```

```python
import functools

import jax
import jax.numpy as jnp
from jax import lax
from jax.experimental import pallas as pl
from jax.experimental.pallas import tpu as pltpu

D_MODEL = 2048
N_HEADS = 8
NOPE = 128
ROPE = 64
QK_DIM = NOPE + ROPE
V_DIM = 128
V_EXT = V_DIM + 16
ATTN_WIDTH = N_HEADS * V_DIM
Q_RANK = 512
KV_RANK = 256
ROPE_BASE = 10000.0
LRU_WIDTH = D_MODEL - ATTN_WIDTH
LRU_BLOCKS = 8
LRU_BD = LRU_WIDTH // LRU_BLOCKS
CONV_W = 4
LRU_C = 8.0
D_FF = 5632
EPS = 1e-6
LOG2_E = 1.4426950408889634

LANES = 128
SUBLANES = 8
HEAD_PAD = 2 * LANES
OFF_Q = Q_RANK
OFF_KV = OFF_Q + KV_RANK
OFF_KG = OFF_KV + LANES
OFF_UX = OFF_KG + LRU_WIDTH
IN_COLS_EXT = OFF_UX + LRU_WIDTH
VMEM_LIMIT = 56 * 1024 * 1024

_F32 = jnp.float32
_BF16 = jnp.bfloat16


def _const_spec(shape):
    return pl.BlockSpec(shape, lambda *_: (0,) * len(shape), pipeline_mode=pl.Buffered(1))


def _params(sem):
    return pltpu.CompilerParams(dimension_semantics=sem, vmem_limit_bytes=VMEM_LIMIT)


def _rms(x, g):
    return x * lax.rsqrt(jnp.mean(x * x, axis=-1, keepdims=True) + EPS) * g


def _lru_rows_spec(tm):
    return pl.BlockSpec((1, tm * LRU_BLOCKS, LRU_BD), lambda b, i: (b, i, 0))


def _in_proj_kernel(x_ref, cos_ref, sin_ref, g_in_ref, w_in_ref, g_qa_ref, w_uq_ref,
                    g_kva_ref, w_uk_ref, w_uvt_ref, gq_ref, gk_ref,
                    q_ref, k_ref, vt_ref, ux_ref, ug_ref):
    x = x_ref[0]
    h = _rms(x, g_in_ref[...]).astype(_BF16)
    u = jnp.dot(h, w_in_ref[...], preferred_element_type=_F32)
    for n in range(LRU_BLOCKS):
        rows = pl.ds(n, x.shape[0], stride=LRU_BLOCKS)
        ux_ref[0, rows, :] = u[:, OFF_KG + n * LRU_BD:OFF_KG + (n + 1) * LRU_BD]
        ug_ref[0, rows, :] = u[:, OFF_UX + n * LRU_BD:OFF_UX + (n + 1) * LRU_BD]

    cos = cos_ref[...]
    sin = sin_ref[...]
    lane = lax.broadcasted_iota(jnp.int32, (1, LANES), 1)
    rope_mask = (lane < ROPE).astype(_F32)
    scale = QK_DIM ** -0.5 * LOG2_E

    def rotary(grp, g_rows):
        return grp * (g_rows[1:2] * cos) + pltpu.roll(grp, ROPE, axis=1) * (g_rows[2:3] * sin)

    def head_rms(nope, grp, extra=None):
        ss = jnp.sum(nope * nope, axis=-1, keepdims=True)
        ss = ss + (jnp.sum(grp * grp * rope_mask, axis=-1, keepdims=True) if extra is None else extra)
        return lax.rsqrt(ss * (1.0 / QK_DIM) + EPS)

    cq = _rms(u[:, :OFF_Q], g_qa_ref[...]).astype(_BF16)
    qf = jnp.dot(cq, w_uq_ref[...], preferred_element_type=_F32)
    gq = gq_ref[...]
    for hd in range(N_HEADS):
        nope = qf[:, hd * HEAD_PAD:hd * HEAD_PAD + NOPE]
        grp = qf[:, hd * HEAD_PAD + NOPE:(hd + 1) * HEAD_PAD]
        r = head_rms(nope, grp) * scale
        q_ref[0, hd, :, :NOPE] = (nope * r * gq[0:1]).astype(_BF16)
        q_ref[0, hd, :, NOPE:] = (rotary(grp, gq) * r).astype(_BF16)

    ckv = _rms(u[:, OFF_Q:OFF_KV], g_kva_ref[...]).astype(_BF16)
    vt = lax.dot_general(w_uvt_ref[...], ckv, (((1,), (1,)), ((), ())), preferred_element_type=_F32)
    ones_rows = (lax.broadcasted_iota(jnp.int32, (V_EXT - V_DIM, vt.shape[1]), 0) == 0).astype(_BF16)
    for hd in range(N_HEADS):
        vt_ref[0, hd, :V_DIM, :] = vt[hd * V_DIM:(hd + 1) * V_DIM].astype(_BF16)
        vt_ref[0, hd, V_DIM:, :] = ones_rows
    kn = jnp.dot(ckv, w_uk_ref[...], preferred_element_type=_F32)
    gk = gk_ref[...]
    kgrp = u[:, OFF_KV:OFF_KG]
    ss_pe = jnp.sum(kgrp * kgrp * rope_mask, axis=-1, keepdims=True)
    krot = rotary(kgrp, gk)
    for hd in range(N_HEADS):
        nope = kn[:, hd * NOPE:(hd + 1) * NOPE]
        r = head_rms(nope, None, ss_pe)
        k_ref[0, hd, :, :NOPE] = (nope * r * gk[0:1]).astype(_BF16)
        k_ref[0, hd, :, NOPE:] = (krot * r).astype(_BF16)


def _in_proj(x, cos, sin, g_in, w_in_ext, g_qa, w_uq_ext, g_kva, w_uk, w_uvt, gq, gk, tm):
    B, S, _ = x.shape
    grid = (B, S // tm)
    tok = lambda w: pl.BlockSpec((1, tm, w), lambda b, i: (b, i, 0))
    head = pl.BlockSpec((1, N_HEADS, tm, HEAD_PAD), lambda b, i: (b, 0, i, 0))
    tab = pl.BlockSpec((tm, LANES), lambda b, i: (i, 0))
    return pl.pallas_call(
        _in_proj_kernel,
        grid=grid,
        in_specs=[tok(D_MODEL), tab, tab,
                  _const_spec((1, D_MODEL)), _const_spec((D_MODEL, IN_COLS_EXT)),
                  _const_spec((1, Q_RANK)), _const_spec((Q_RANK, N_HEADS * HEAD_PAD)),
                  _const_spec((1, KV_RANK)), _const_spec((KV_RANK, N_HEADS * NOPE)),
                  _const_spec((ATTN_WIDTH, KV_RANK)),
                  _const_spec((3, LANES)), _const_spec((3, LANES))],
        out_specs=[head, head, pl.BlockSpec((1, N_HEADS, V_EXT, tm), lambda b, i: (b, 0, 0, i)),
                   _lru_rows_spec(tm), _lru_rows_spec(tm)],
        out_shape=[jax.ShapeDtypeStruct((B, N_HEADS, S, HEAD_PAD), _BF16),
                   jax.ShapeDtypeStruct((B, N_HEADS, S, HEAD_PAD), _BF16),
                   jax.ShapeDtypeStruct((B, N_HEADS, V_EXT, S), _BF16),
                   jax.ShapeDtypeStruct((B, S * LRU_BLOCKS, LRU_BD), _F32),
                   jax.ShapeDtypeStruct((B, S * LRU_BLOCKS, LRU_BD), _F32)],
        compiler_params=_params(("parallel", "parallel")),
        name="in_proj",
    )(x, cos, sin, g_in, w_in_ext, g_qa, w_uq_ext, g_kva, w_uk, w_uvt, gq, gk)


def _attention_kernel(q_ref, k_ref, vt_ref, o_ref, s_scr, qt_scr, *, tk, unroll):
    n = k_ref.shape[2] // tk
    tq = q_ref.shape[2]
    qt_scr[...] = q_ref[0, 0].T

    def scores(j, slot):
        rows = pl.ds(pl.multiple_of(j * tk, tk), tk)
        s_scr[slot] = jnp.dot(k_ref[0, 0, rows, :], qt_scr[...],
                              preferred_element_type=_F32)

    def update(j, slot, m, acc):
        cols = pl.ds(pl.multiple_of(j * tk, tk), tk)
        m_new = jnp.maximum(m, jnp.max(s_scr[slot], axis=0, keepdims=True))
        p = jnp.exp2(s_scr[slot] - m_new).astype(_BF16)
        acc = jnp.exp2(m - m_new) * acc + jnp.dot(vt_ref[0, 0, :, cols], p,
                                                  preferred_element_type=_F32)
        return m_new, acc

    def body(jj, carry):
        m, acc = carry
        j = 2 * jj
        scores(j + 1, 1)
        m, acc = update(j, 0, m, acc)
        scores(j + 2, 0)
        return update(j + 1, 1, m, acc)

    scores(0, 0)
    carry = (jnp.full((1, tq), -jnp.inf, _F32), jnp.zeros((V_EXT, tq), _F32))
    m, acc = lax.fori_loop(0, n // 2 - 1, body, carry, unroll=unroll)
    scores(n - 1, 1)
    m, acc = update(n - 2, 0, m, acc)
    m, acc = update(n - 1, 1, m, acc)
    out = acc[:V_DIM] / acc[V_DIM:V_DIM + 1]
    o_ref[0] = out.T.astype(o_ref.dtype)


def _attention(q, k, vt, tq, tk):
    B, H, S, _ = q.shape
    pairs = S // tk // 2 - 1
    assert S % (2 * tk) == 0 and pairs >= 0
    unroll = max(u for u in range(1, 6) if pairs % u == 0) if pairs else 1
    return pl.pallas_call(
        functools.partial(_attention_kernel, tk=tk, unroll=unroll),
        grid=(B, H, S // tq),
        in_specs=[pl.BlockSpec((1, 1, tq, HEAD_PAD), lambda b, h, i: (b, h, i, 0)),
                  pl.BlockSpec((1, 1, S, HEAD_PAD), lambda b, h, i: (b, h, 0, 0)),
                  pl.BlockSpec((1, 1, V_EXT, S), lambda b, h, i: (b, h, 0, 0))],
        out_specs=pl.BlockSpec((1, tq, V_DIM), lambda b, h, i: (b, i, h)),
        out_shape=jax.ShapeDtypeStruct((B, S, ATTN_WIDTH), _BF16),
        scratch_shapes=[pltpu.VMEM((2, tk, tq), _F32), pltpu.VMEM((HEAD_PAD, tq), _BF16)],
        compiler_params=_params(("parallel", "parallel", "arbitrary")),
        name="attention",
    )(q, k, vt)


def _lru_gates(ux_ref, prev_ref, next_ref, cw_ref, cb_ref, wg_ref, bg_ref, lam_ref,
               xc_scr, a_scr, b_scr, *, tt, first, last):
    x = ux_ref[0]
    prev = jnp.where(first, 0.0, prev_ref[0])
    nxt = jnp.where(last, 0.0, next_ref[0])
    xe = jnp.concatenate([prev[SUBLANES - 1:], x, nxt[:2]], axis=0)
    cw = cw_ref[...]
    xc = cb_ref[...][None]
    for tap in range(CONV_W):
        xc = xc + xe[tap:tap + tt] * cw[tap][None]
    xc_scr[...] = xc.reshape(tt * LRU_BLOCKS, LRU_BD)
    for n in range(LRU_BLOCKS):
        rows = pl.ds(n, tt, stride=LRU_BLOCKS)
        xn = xc_scr[rows, :]
        z = jnp.dot(xn.astype(_BF16), wg_ref[n], preferred_element_type=_F32) + bg_ref[n]
        sig = 0.5 * jnp.tanh(0.5 * z) + 0.5
        r = sig[:, :LRU_BD]
        i = sig[:, LRU_BD:]
        log_a = (-LRU_C) * r * jax.nn.softplus(-lam_ref[n])
        a = jnp.exp(log_a)
        a_scr[rows, :] = a
        b_scr[rows, :] = jnp.sqrt(jnp.tanh(-log_a) * (1.0 + a * a)) * (i * xn)


def _lru_fwd_kernel(ux_ref, prev_ref, next_ref, cw_ref, cb_ref, wg_ref, bg_ref, lam_ref,
                    hf_ref, xc_scr, a_scr, b_scr, h_scr, *, tt):
    i = pl.program_id(1)
    _lru_gates(ux_ref, prev_ref, next_ref, cw_ref, cb_ref, wg_ref, bg_ref, lam_ref,
               xc_scr, a_scr, b_scr, tt=tt, first=i == 0, last=i == pl.num_programs(1) - 1)

    @pl.when(i == 0)
    def _():
        h_scr[...] = jnp.zeros(h_scr.shape, _F32)

    def step(t, h):
        rows = pl.ds(pl.multiple_of(t * LRU_BLOCKS, LRU_BLOCKS), LRU_BLOCKS)
        h = a_scr[rows, :] * h + b_scr[rows, :]
        hf_ref[0, t] = h
        return h

    h_scr[...] = lax.fori_loop(0, tt, step, h_scr[...], unroll=8)


def _lru_bwd_kernel(ux_ref, prev_ref, next_ref, cw_ref, cb_ref, wg_ref, bg_ref, lam_ref,
                    hf_ref, ug_ref, out_ref, xc_scr, a_scr, b_scr, h_scr, *, tt):
    i = pl.program_id(1)
    _lru_gates(ux_ref, prev_ref, next_ref, cw_ref, cb_ref, wg_ref, bg_ref, lam_ref,
               xc_scr, a_scr, b_scr, tt=tt, first=i == pl.num_programs(1) - 1, last=i == 0)

    @pl.when(i == 0)
    def _():
        h_scr[...] = jnp.zeros(h_scr.shape, _F32)

    def step(k, h):
        t = tt - 1 - k
        rows = pl.ds(pl.multiple_of(t * LRU_BLOCKS, LRU_BLOCKS), LRU_BLOCKS)
        h = a_scr[rows, :] * h + b_scr[rows, :]
        out_ref[0, t] = (hf_ref[0, t] + h) * jax.nn.gelu(ug_ref[0, t])
        return h

    h_scr[...] = lax.fori_loop(0, tt, step, h_scr[...], unroll=8)


def _lru(ux, ug, conv_w, conv_b, wg, bg, lam, tt):
    B, S = ux.shape[:2]
    nt = S // tt
    nh = tt // SUBLANES
    tile4 = (1, tt, LRU_BLOCKS, LRU_BD)
    halo4 = (1, SUBLANES, LRU_BLOCKS, LRU_BD)
    last_halo = S // SUBLANES - 1
    scratch = [pltpu.VMEM((tt * LRU_BLOCKS, LRU_BD), _F32) for _ in range(3)]
    scratch.append(pltpu.VMEM((LRU_BLOCKS, LRU_BD), _F32))
    consts = [_const_spec((CONV_W, LRU_BLOCKS, LRU_BD)), _const_spec((LRU_BLOCKS, LRU_BD)),
              _const_spec((LRU_BLOCKS, LRU_BD, 2 * LRU_BD)),
              _const_spec((LRU_BLOCKS, 1, 2 * LRU_BD)), _const_spec((LRU_BLOCKS, 1, LRU_BD))]

    def specs(tile_of):
        return [pl.BlockSpec(tile4, lambda b, i: (b, tile_of(i), 0, 0)),
                pl.BlockSpec(halo4, lambda b, i: (b, jnp.maximum(tile_of(i) * nh - 1, 0), 0, 0)),
                pl.BlockSpec(halo4, lambda b, i: (b, jnp.minimum((tile_of(i) + 1) * nh, last_halo), 0, 0))]

    fwd_tile = lambda i: i
    bwd_tile = lambda i: nt - 1 - i
    shape4 = jax.ShapeDtypeStruct((B, S, LRU_BLOCKS, LRU_BD), _F32)
    hf = pl.pallas_call(
        functools.partial(_lru_fwd_kernel, tt=tt),
        grid=(B, nt),
        in_specs=specs(fwd_tile) + consts,
        out_specs=pl.BlockSpec(tile4, lambda b, i: (b, i, 0, 0)),
        out_shape=shape4,
        scratch_shapes=scratch,
        compiler_params=_params(("parallel", "arbitrary")),
        name="lru_forward",
    )(ux, ux, ux, conv_w, conv_b, wg[0], bg[0], lam[0])
    bwd_spec = pl.BlockSpec(tile4, lambda b, i: (b, nt - 1 - i, 0, 0))
    return pl.pallas_call(
        functools.partial(_lru_bwd_kernel, tt=tt),
        grid=(B, nt),
        in_specs=specs(bwd_tile) + consts + [bwd_spec, bwd_spec],
        out_specs=bwd_spec,
        out_shape=shape4,
        scratch_shapes=scratch,
        compiler_params=_params(("parallel", "arbitrary")),
        name="lru_backward",
    )(ux, ux, ux, conv_w, conv_b, wg[1], bg[1], lam[1], hf, ug)


def _out_proj_kernel(x_ref, attn_ref, lru_ref, ga_ref, gl_ref, wa_ref, wl_ref, y_ref):
    a = _rms(attn_ref[0].astype(_F32), ga_ref[...]).astype(_BF16)
    tm = x_ref.shape[1]
    lru = jnp.concatenate([lru_ref[0, pl.ds(n, tm, stride=LRU_BLOCKS), :] for n in range(LRU_BLOCKS)],
                          axis=1)
    l = _rms(lru, gl_ref[...]).astype(_BF16)
    y = jnp.dot(a, wa_ref[...], preferred_element_type=_F32)
    y = y + jnp.dot(l, wl_ref[...], preferred_element_type=_F32)
    y_ref[0] = x_ref[0] + y


def _out_proj(x, attn, lru, ga, gl, w_out_a, w_out_l, tm):
    B, S, _ = x.shape
    tok = lambda w: pl.BlockSpec((1, tm, w), lambda b, i: (b, i, 0))
    return pl.pallas_call(
        _out_proj_kernel,
        grid=(B, S // tm),
        in_specs=[tok(D_MODEL), tok(ATTN_WIDTH), _lru_rows_spec(tm),
                  _const_spec((1, ATTN_WIDTH)), _const_spec((1, LRU_WIDTH)),
                  _const_spec((ATTN_WIDTH, D_MODEL)), _const_spec((LRU_WIDTH, D_MODEL))],
        out_specs=tok(D_MODEL),
        out_shape=jax.ShapeDtypeStruct(x.shape, _F32),
        compiler_params=_params(("parallel", "parallel")),
        name="out_proj",
    )(x, attn, lru, ga, gl, w_out_a, w_out_l)


def _ffn_kernel(x_ref, g_ref, wg_ref, wu_ref, wd_ref, y_ref, h_scr):
    @pl.when(pl.program_id(2) == 0)
    def _():
        x = x_ref[0]
        h_scr[...] = _rms(x, g_ref[...]).astype(_BF16)
        y_ref[0] = x

    h = h_scr[...]
    gate = jnp.dot(h, wg_ref[...], preferred_element_type=_F32)
    up = jnp.dot(h, wu_ref[...], preferred_element_type=_F32)
    act = (jax.nn.silu(gate) * up).astype(_BF16)
    y_ref[0] += jnp.dot(act, wd_ref[...], preferred_element_type=_F32)


def _ffn(x, g, w_gate, w_up, w_down, tm, tf):
    B, S, _ = x.shape
    tok = pl.BlockSpec((1, tm, D_MODEL), lambda b, i, j: (b, i, 0))
    return pl.pallas_call(
        _ffn_kernel,
        grid=(B, S // tm, D_FF // tf),
        in_specs=[tok, pl.BlockSpec((1, D_MODEL), lambda b, i, j: (0, 0)),
                  pl.BlockSpec((D_MODEL, tf), lambda b, i, j: (0, j)),
                  pl.BlockSpec((D_MODEL, tf), lambda b, i, j: (0, j)),
                  pl.BlockSpec((tf, D_MODEL), lambda b, i, j: (j, 0))],
        out_specs=tok,
        out_shape=jax.ShapeDtypeStruct(x.shape, _F32),
        scratch_shapes=[pltpu.VMEM((tm, D_MODEL), _BF16)],
        compiler_params=_params(("parallel", "parallel", "arbitrary")),
        name="ffn",
    )(x, g, w_gate, w_up, w_down)


def _swap_halves(w):
    half = ROPE // 2
    return jnp.concatenate([-w[..., half:], w[..., :half]], axis=-1)


def _swap_gain(g):
    half = ROPE // 2
    return jnp.concatenate([g[half:], g[:half]])


def _gain_rows(g):
    pad = jnp.zeros((LANES - ROPE,), _F32)
    pe = g[NOPE:]
    return jnp.stack([g[:NOPE], jnp.concatenate([pe, pad]),
                      jnp.concatenate([_swap_gain(pe), pad])])


def _prepare(attn_norm_g, w_in, q_a_norm_g, w_uq, kv_a_norm_g, w_ukv, q_norm_g, k_norm_g,
             conv_w, conv_b, w_rg_r, b_rg_r, w_rg_i, b_rg_i, lru_lambda,
             attn_out_norm_g, lru_out_norm_g, w_out, ffn_norm_g, w_gate, w_up, w_down):
    k_pe = w_in[:, OFF_KV:OFF_KV + ROPE]
    w_in_ext = jnp.concatenate(
        [w_in[:, :OFF_KV + ROPE], _swap_halves(k_pe), w_in[:, OFF_KV + ROPE:]], axis=1).astype(_BF16)
    wq = w_uq.reshape(Q_RANK, N_HEADS, QK_DIM)
    w_uq_ext = jnp.concatenate([wq, _swap_halves(wq[..., NOPE:])], axis=-1)
    w_uq_ext = w_uq_ext.reshape(Q_RANK, N_HEADS * HEAD_PAD).astype(_BF16)
    wkv = w_ukv.reshape(KV_RANK, N_HEADS, NOPE + V_DIM)
    w_uk = wkv[..., :NOPE].reshape(KV_RANK, N_HEADS * NOPE).astype(_BF16)
    w_uvt = wkv[..., NOPE:].reshape(KV_RANK, ATTN_WIDTH).T.astype(_BF16)
    wg = jnp.concatenate([w_rg_r, w_rg_i], axis=-1).astype(_BF16)
    bg = jnp.concatenate([b_rg_r, b_rg_i], axis=-1)[:, :, None, :]
    lam = lru_lambda.reshape(2, LRU_BLOCKS, 1, LRU_BD)
    return dict(
        g_in=attn_norm_g[None], w_in_ext=w_in_ext, g_qa=q_a_norm_g[None], w_uq_ext=w_uq_ext,
        g_kva=kv_a_norm_g[None], w_uk=w_uk, w_uvt=w_uvt,
        gq=_gain_rows(q_norm_g), gk=_gain_rows(k_norm_g),
        conv_w=conv_w.reshape(CONV_W, LRU_BLOCKS, LRU_BD), conv_b=conv_b.reshape(LRU_BLOCKS, LRU_BD),
        wg=wg, bg=bg, lam=lam,
        ga=attn_out_norm_g[None], gl=lru_out_norm_g[None],
        w_out_a=w_out[:ATTN_WIDTH].astype(_BF16), w_out_l=w_out[ATTN_WIDTH:].astype(_BF16),
        g_ffn=ffn_norm_g[None], w_gate=w_gate.astype(_BF16), w_up=w_up.astype(_BF16),
        w_down=w_down.astype(_BF16))


def _rope_tables(S):
    half = ROPE // 2
    inv_freq = ROPE_BASE ** (-2.0 * jnp.arange(half, dtype=_F32) / ROPE)
    ang = jnp.arange(S, dtype=_F32)[:, None] * inv_freq[None, :]
    pad = jnp.zeros((S, LANES - ROPE), _F32)
    cos = jnp.concatenate([jnp.cos(ang), jnp.cos(ang), pad], axis=1)
    sin = jnp.concatenate([jnp.sin(ang), jnp.sin(ang), pad], axis=1)
    return cos, sin


def _tiles(S):
    return dict(in_proj=min(256, S), attn_q=min(512, S), attn_k=min(512, S // 2),
                lru=min(256, S), out_proj=min(512, S), ffn=min(512, S), ffn_cols=512)


def _encoder_layer(x, p):
    B, S, _ = x.shape
    t = _tiles(S)
    cos, sin = _rope_tables(S)
    q, k, vt, ux, ug = _in_proj(x, cos, sin, p["g_in"], p["w_in_ext"], p["g_qa"], p["w_uq_ext"],
                               p["g_kva"], p["w_uk"], p["w_uvt"], p["gq"], p["gk"], t["in_proj"])
    attn = _attention(q, k, vt, t["attn_q"], t["attn_k"])
    shape4 = (B, S, LRU_BLOCKS, LRU_BD)
    lru = _lru(ux.reshape(shape4), ug.reshape(shape4), p["conv_w"], p["conv_b"],
               p["wg"], p["bg"], p["lam"], t["lru"]).reshape(B, S * LRU_BLOCKS, LRU_BD)
    x1 = _out_proj(x, attn, lru, p["ga"], p["gl"], p["w_out_a"], p["w_out_l"], t["out_proj"])
    return _ffn(x1, p["g_ffn"], p["w_gate"], p["w_up"], p["w_down"], t["ffn"], t["ffn_cols"])


def kernel(x_prompt, x_sample, attn_norm_g, w_in, q_a_norm_g, w_uq, kv_a_norm_g, w_ukv, q_norm_g, k_norm_g, conv_w, conv_b, w_rg_r, b_rg_r, w_rg_i, b_rg_i, lru_lambda, attn_out_norm_g, lru_out_norm_g, w_out, ffn_norm_g, w_gate, w_up, w_down):
    depth = attn_norm_g.shape[0]
    weights = (attn_norm_g, w_in, q_a_norm_g, w_uq, kv_a_norm_g, w_ukv, q_norm_g, k_norm_g,
               conv_w, conv_b, w_rg_r, b_rg_r, w_rg_i, b_rg_i, lru_lambda,
               attn_out_norm_g, lru_out_norm_g, w_out, ffn_norm_g, w_gate, w_up, w_down)
    y_prompt, y_sample = x_prompt, x_sample
    for layer in range(depth):
        p = _prepare(*(w[layer] for w in weights))
        y_prompt = _encoder_layer(y_prompt, p)
        y_sample = _encoder_layer(y_sample, p)
    return (y_prompt, y_sample)
```

```python
import functools

import jax
import jax.numpy as jnp
from jax import lax
from jax.experimental import pallas as pl
from jax.experimental.pallas import tpu as pltpu

D_MODEL = 2048
N_HEADS = 8
NOPE = 128
ROPE = 64
QK_DIM = NOPE + ROPE
V_DIM = 128
V_EXT = V_DIM + 16
ATTN_WIDTH = N_HEADS * V_DIM
Q_RANK = 512
KV_RANK = 256
ROPE_BASE = 10000.0
LRU_WIDTH = D_MODEL - ATTN_WIDTH
LRU_BLOCKS = 8
LRU_BD = LRU_WIDTH // LRU_BLOCKS
CONV_W = 4
LRU_C = 8.0
D_FF = 5632
EPS = 1e-6
LOG2_E = 1.4426950408889634

LANES = 128
SUBLANES = 8
HEAD_PAD = 2 * LANES
OFF_Q = Q_RANK
OFF_KV = OFF_Q + KV_RANK
OFF_KG = OFF_KV + LANES
OFF_UX = OFF_KG + LRU_WIDTH
IN_COLS_EXT = OFF_UX + LRU_WIDTH
VMEM_LIMIT = 56 * 1024 * 1024

_F32 = jnp.float32
_BF16 = jnp.bfloat16


def _const_spec(shape):
    return pl.BlockSpec(shape, lambda *_: (0,) * len(shape), pipeline_mode=pl.Buffered(1))


def _params(sem):
    return pltpu.CompilerParams(dimension_semantics=sem, vmem_limit_bytes=VMEM_LIMIT)


def _rms(x, g):
    return x * lax.rsqrt(jnp.mean(x * x, axis=-1, keepdims=True) + EPS) * g


def _lru_rows_spec(tm):
    return pl.BlockSpec((1, tm * LRU_BLOCKS, LRU_BD), lambda b, i: (b, i, 0))


def _in_proj_kernel(x_ref, cos_ref, sin_ref, g_in_ref, w_in_ref, g_qa_ref, w_uq_ref,
                    g_kva_ref, w_uk_ref, w_uvt_ref, gq_ref, gk_ref,
                    q_ref, k_ref, vt_ref, ux_ref, ug_ref):
    x = x_ref[0]
    h = _rms(x, g_in_ref[...]).astype(_BF16)
    u = jnp.dot(h, w_in_ref[:, :OFF_KG], preferred_element_type=_F32)

    cos = cos_ref[...]
    sin = sin_ref[...]
    lane = lax.broadcasted_iota(jnp.int32, (1, LANES), 1)
    rope_mask = (lane < ROPE).astype(_F32)
    scale = QK_DIM ** -0.5 * LOG2_E

    def rotary(grp, g_rows):
        return grp * (g_rows[1:2] * cos) + pltpu.roll(grp, ROPE, axis=1) * (g_rows[2:3] * sin)

    def head_rms(nope, grp, extra=None):
        ss = jnp.sum(nope * nope, axis=-1, keepdims=True)
        ss = ss + (jnp.sum(grp * grp * rope_mask, axis=-1, keepdims=True) if extra is None else extra)
        return lax.rsqrt(ss * (1.0 / QK_DIM) + EPS)

    cq = _rms(u[:, :OFF_Q], g_qa_ref[...]).astype(_BF16)
    qf = jnp.dot(cq, w_uq_ref[...], preferred_element_type=_F32)
    gq = gq_ref[...]
    for hd in range(N_HEADS):
        nope = qf[:, hd * HEAD_PAD:hd * HEAD_PAD + NOPE]
        grp = qf[:, hd * HEAD_PAD + NOPE:(hd + 1) * HEAD_PAD]
        r = head_rms(nope, grp) * scale
        q_ref[0, hd, :, :NOPE] = (nope * r * gq[0:1]).astype(_BF16)
        q_ref[0, hd, :, NOPE:] = (rotary(grp, gq) * r).astype(_BF16)

    ckv = _rms(u[:, OFF_Q:OFF_KV], g_kva_ref[...]).astype(_BF16)
    vt = lax.dot_general(w_uvt_ref[...], ckv, (((1,), (1,)), ((), ())), preferred_element_type=_F32)
    ones_rows = (lax.broadcasted_iota(jnp.int32, (V_EXT - V_DIM, vt.shape[1]), 0) == 0).astype(_BF16)
    for hd in range(N_HEADS):
        vt_ref[0, hd, :V_DIM, :] = vt[hd * V_DIM:(hd + 1) * V_DIM].astype(_BF16)
        vt_ref[0, hd, V_DIM:, :] = ones_rows
    kn = jnp.dot(ckv, w_uk_ref[...], preferred_element_type=_F32)
    gk = gk_ref[...]
    kgrp = u[:, OFF_KV:OFF_KG]
    ss_pe = jnp.sum(kgrp * kgrp * rope_mask, axis=-1, keepdims=True)
    krot = rotary(kgrp, gk)
    for hd in range(N_HEADS):
        nope = kn[:, hd * NOPE:(hd + 1) * NOPE]
        r = head_rms(nope, None, ss_pe)
        k_ref[0, hd, :, :NOPE] = (nope * r * gk[0:1]).astype(_BF16)
        k_ref[0, hd, :, NOPE:] = (krot * r).astype(_BF16)

    ul = jnp.dot(h, w_in_ref[:, OFF_KG:], preferred_element_type=_F32)
    for n in range(LRU_BLOCKS):
        rows = pl.ds(n, x.shape[0], stride=LRU_BLOCKS)
        ux_ref[0, rows, :] = ul[:, n * LRU_BD:(n + 1) * LRU_BD]
        ug_ref[0, rows, :] = ul[:, LRU_WIDTH + n * LRU_BD:LRU_WIDTH + (n + 1) * LRU_BD]


def _in_proj(x, cos, sin, g_in, w_in_ext, g_qa, w_uq_ext, g_kva, w_uk, w_uvt, gq, gk, tm):
    B, S, _ = x.shape
    grid = (B, S // tm)
    tok = lambda w: pl.BlockSpec((1, tm, w), lambda b, i: (b, i, 0))
    head = pl.BlockSpec((1, N_HEADS, tm, HEAD_PAD), lambda b, i: (b, 0, i, 0))
    tab = pl.BlockSpec((tm, LANES), lambda b, i: (i, 0))
    return pl.pallas_call(
        _in_proj_kernel,
        grid=grid,
        in_specs=[tok(D_MODEL), tab, tab,
                  _const_spec((1, D_MODEL)), _const_spec((D_MODEL, IN_COLS_EXT)),
                  _const_spec((1, Q_RANK)), _const_spec((Q_RANK, N_HEADS * HEAD_PAD)),
                  _const_spec((1, KV_RANK)), _const_spec((KV_RANK, N_HEADS * NOPE)),
                  _const_spec((ATTN_WIDTH, KV_RANK)),
                  _const_spec((3, LANES)), _const_spec((3, LANES))],
        out_specs=[head, head, pl.BlockSpec((1, N_HEADS, V_EXT, tm), lambda b, i: (b, 0, 0, i)),
                   _lru_rows_spec(tm), _lru_rows_spec(tm)],
        out_shape=[jax.ShapeDtypeStruct((B, N_HEADS, S, HEAD_PAD), _BF16),
                   jax.ShapeDtypeStruct((B, N_HEADS, S, HEAD_PAD), _BF16),
                   jax.ShapeDtypeStruct((B, N_HEADS, V_EXT, S), _BF16),
                   jax.ShapeDtypeStruct((B, S * LRU_BLOCKS, LRU_BD), _F32),
                   jax.ShapeDtypeStruct((B, S * LRU_BLOCKS, LRU_BD), _F32)],
        compiler_params=_params(("parallel", "parallel")),
        name="in_proj",
    )(x, cos, sin, g_in, w_in_ext, g_qa, w_uq_ext, g_kva, w_uk, w_uvt, gq, gk)


def _attention_kernel(q_ref, k_ref, vt_ref, o_ref, s_scr, qt_scr, *, tq, tk, unroll):
    n = k_ref.shape[2] // tk
    nq = q_ref.shape[2] // tq

    def load_qt(qi, slot):
        qt_scr[slot] = q_ref[0, 0, pl.ds(pl.multiple_of(qi * tq, tq), tq), :].T

    def scores(qslot, j, slot):
        rows = pl.ds(pl.multiple_of(j * tk, tk), tk)
        s = jnp.dot(k_ref[0, 0, rows, :], qt_scr[qslot], preferred_element_type=_F32)
        s_scr[slot] = s
        return jnp.max(s, axis=0, keepdims=True)

    def update(j, slot, m, acc, cmax):
        cols = pl.ds(pl.multiple_of(j * tk, tk), tk)
        m_new = jnp.maximum(m, cmax)
        p = jnp.exp2(s_scr[slot] - m_new).astype(_BF16)
        acc = jnp.exp2(m - m_new) * acc + jnp.dot(vt_ref[0, 0, :, cols], p,
                                                  preferred_element_type=_F32)
        return m_new, acc

    def tile(qi, c0):
        qslot = qi % 2

        def body(jj, carry):
            m, acc, c0 = carry
            j = 2 * jj
            c1 = scores(qslot, j + 1, 1)
            m, acc = update(j, 0, m, acc, c0)
            c0 = scores(qslot, j + 2, 0)
            m, acc = update(j + 1, 1, m, acc, c1)
            return m, acc, c0

        carry = (jnp.full((1, tq), -jnp.inf, _F32), jnp.zeros((V_EXT, tq), _F32), c0)
        m, acc, c0 = lax.fori_loop(0, n // 2 - 1, body, carry, unroll=unroll)
        c1 = scores(qslot, n - 1, 1)
        m, acc = update(n - 2, 0, m, acc, c0)
        load_qt(jnp.minimum(qi + 1, nq - 1), 1 - qslot)
        c_next = scores(1 - qslot, 0, 0)
        m, acc = update(n - 1, 1, m, acc, c1)
        out = acc[:V_DIM] / acc[V_DIM:V_DIM + 1]
        o_ref[0, pl.ds(pl.multiple_of(qi * tq, tq), tq), :] = out.T.astype(o_ref.dtype)
        return c_next

    load_qt(0, 0)
    lax.fori_loop(0, nq, tile, scores(0, 0, 0))


def _attention(q, k, vt, tq, tk, span):
    B, H, S, _ = q.shape
    pairs = S // tk // 2 - 1
    assert S % (2 * tk) == 0 and pairs >= 0 and S % span == 0 and span % tq == 0
    unroll = max(u for u in range(1, 6) if pairs % u == 0) if pairs else 1
    return pl.pallas_call(
        functools.partial(_attention_kernel, tq=tq, tk=tk, unroll=unroll),
        grid=(B, H, S // span),
        in_specs=[pl.BlockSpec((1, 1, span, HEAD_PAD), lambda b, h, i: (b, h, i, 0)),
                  pl.BlockSpec((1, 1, S, HEAD_PAD), lambda b, h, i: (b, h, 0, 0)),
                  pl.BlockSpec((1, 1, V_EXT, S), lambda b, h, i: (b, h, 0, 0))],
        out_specs=pl.BlockSpec((1, span, V_DIM), lambda b, h, i: (b, i, h)),
        out_shape=jax.ShapeDtypeStruct((B, S, ATTN_WIDTH), _BF16),
        scratch_shapes=[pltpu.VMEM((2, tk, tq), _F32), pltpu.VMEM((2, HEAD_PAD, tq), _BF16)],
        compiler_params=_params(("parallel", "parallel", "arbitrary")),
        name="attention",
    )(q, k, vt)


def _lru_gates(ux_ref, prev_ref, next_ref, cw_ref, cb_ref, wg_ref, bg_ref, lam_ref,
               xc_scr, a_scr, b_scr, *, tt, first, last):
    x = ux_ref[0]
    prev = jnp.where(first, 0.0, prev_ref[0])
    nxt = jnp.where(last, 0.0, next_ref[0])
    xe = jnp.concatenate([prev[SUBLANES - 1:], x, nxt[:2]], axis=0)
    cw = cw_ref[...]
    xc = cb_ref[...][None]
    for tap in range(CONV_W):
        xc = xc + xe[tap:tap + tt] * cw[tap][None]
    xc_scr[...] = xc.reshape(tt * LRU_BLOCKS, LRU_BD)
    k1 = LRU_C * jax.nn.softplus(-lam_ref[...])
    k2 = -LOG2_E * k1
    for n in range(LRU_BLOCKS):
        rows = pl.ds(n, tt, stride=LRU_BLOCKS)
        xn = xc_scr[rows, :]
        z = jnp.dot(xn.astype(_BF16), wg_ref[n], preferred_element_type=_F32) + bg_ref[n]
        sig = 0.5 * jnp.tanh(z) + 0.5
        r = sig[:, :LRU_BD]
        i = sig[:, LRU_BD:]
        a = jnp.exp2(r * k2[n])
        a_scr[rows, :] = a
        y = jnp.tanh(r * k1[n]) * (1.0 + a * a)
        b_scr[rows, :] = jnp.where(y > 0.0, y * lax.rsqrt(y), 0.0) * (i * xn)


def _tok(t):
    return pl.ds(pl.multiple_of(t * LRU_BLOCKS, LRU_BLOCKS), LRU_BLOCKS)


def _lru_fwd_kernel(ux_ref, prev_ref, next_ref, cw_ref, cb_ref, wg_ref, bg_ref, lam_ref,
                    hf_ref, xc_scr, a_scr, b_scr, h_scr, *, tt):
    i = pl.program_id(1)
    _lru_gates(ux_ref, prev_ref, next_ref, cw_ref, cb_ref, wg_ref, bg_ref, lam_ref,
               xc_scr, a_scr, b_scr, tt=tt, first=i == 0, last=i == pl.num_programs(1) - 1)

    @pl.when(i == 0)
    def _():
        h_scr[...] = jnp.zeros(h_scr.shape, _F32)

    def step(k, h):
        t = 2 * k
        a0, b0, a1, b1 = a_scr[_tok(t), :], b_scr[_tok(t), :], a_scr[_tok(t + 1), :], b_scr[_tok(t + 1), :]
        h1 = (a1 * a0) * h + (a1 * b0 + b1)
        hf_ref[0, t] = a0 * h + b0
        hf_ref[0, t + 1] = h1
        return h1

    h_scr[...] = lax.fori_loop(0, tt // 2, step, h_scr[...], unroll=4)


def _lru_bwd_kernel(ux_ref, prev_ref, next_ref, cw_ref, cb_ref, wg_ref, bg_ref, lam_ref,
                    hf_ref, ug_ref, out_ref, xc_scr, a_scr, b_scr, h_scr, *, tt):
    i = pl.program_id(1)
    _lru_gates(ux_ref, prev_ref, next_ref, cw_ref, cb_ref, wg_ref, bg_ref, lam_ref,
               xc_scr, a_scr, b_scr, tt=tt, first=i == pl.num_programs(1) - 1, last=i == 0)

    @pl.when(i == 0)
    def _():
        h_scr[...] = jnp.zeros(h_scr.shape, _F32)

    def step(k, h):
        t = tt - 1 - 2 * k
        a0, b0, a1, b1 = a_scr[_tok(t), :], b_scr[_tok(t), :], a_scr[_tok(t - 1), :], b_scr[_tok(t - 1), :]
        h1 = (a1 * a0) * h + (a1 * b0 + b1)
        out_ref[0, t] = (hf_ref[0, t] + (a0 * h + b0)) * jax.nn.gelu(ug_ref[0, t])
        out_ref[0, t - 1] = (hf_ref[0, t - 1] + h1) * jax.nn.gelu(ug_ref[0, t - 1])
        return h1

    h_scr[...] = lax.fori_loop(0, tt // 2, step, h_scr[...], unroll=4)


def _lru(ux, ug, conv_w, conv_b, wg, bg, lam, tt):
    B, S = ux.shape[:2]
    nt = S // tt
    nh = tt // SUBLANES
    tile4 = (1, tt, LRU_BLOCKS, LRU_BD)
    halo4 = (1, SUBLANES, LRU_BLOCKS, LRU_BD)
    last_halo = S // SUBLANES - 1
    scratch = [pltpu.VMEM((tt * LRU_BLOCKS, LRU_BD), _F32) for _ in range(3)]
    scratch.append(pltpu.VMEM((LRU_BLOCKS, LRU_BD), _F32))
    consts = [_const_spec((CONV_W, LRU_BLOCKS, LRU_BD)), _const_spec((LRU_BLOCKS, LRU_BD)),
              _const_spec((LRU_BLOCKS, LRU_BD, 2 * LRU_BD)),
              _const_spec((LRU_BLOCKS, 1, 2 * LRU_BD)), _const_spec((LRU_BLOCKS, 1, LRU_BD))]

    def specs(tile_of):
        return [pl.BlockSpec(tile4, lambda b, i: (b, tile_of(i), 0, 0)),
                pl.BlockSpec(halo4, lambda b, i: (b, jnp.maximum(tile_of(i) * nh - 1, 0), 0, 0)),
                pl.BlockSpec(halo4, lambda b, i: (b, jnp.minimum((tile_of(i) + 1) * nh, last_halo), 0, 0))]

    fwd_tile = lambda i: i
    bwd_tile = lambda i: nt - 1 - i
    shape4 = jax.ShapeDtypeStruct((B, S, LRU_BLOCKS, LRU_BD), _F32)
    hf = pl.pallas_call(
        functools.partial(_lru_fwd_kernel, tt=tt),
        grid=(B, nt),
        in_specs=specs(fwd_tile) + consts,
        out_specs=pl.BlockSpec(tile4, lambda b, i: (b, i, 0, 0)),
        out_shape=shape4,
        scratch_shapes=scratch,
        compiler_params=_params(("parallel", "arbitrary")),
        name="lru_forward",
    )(ux, ux, ux, conv_w, conv_b, wg[0], bg[0], lam[0])
    bwd_spec = pl.BlockSpec(tile4, lambda b, i: (b, nt - 1 - i, 0, 0))
    return pl.pallas_call(
        functools.partial(_lru_bwd_kernel, tt=tt),
        grid=(B, nt),
        in_specs=specs(bwd_tile) + consts + [bwd_spec, bwd_spec],
        out_specs=bwd_spec,
        out_shape=shape4,
        scratch_shapes=scratch,
        compiler_params=_params(("parallel", "arbitrary")),
        name="lru_backward",
    )(ux, ux, ux, conv_w, conv_b, wg[1], bg[1], lam[1], hf, ug)


def _out_proj_kernel(x_ref, attn_ref, lru_ref, ga_ref, gl_ref, wa_ref, wl_ref, y_ref):
    a = _rms(attn_ref[0].astype(_F32), ga_ref[...]).astype(_BF16)
    tm = x_ref.shape[1]
    lru = jnp.concatenate([lru_ref[0, pl.ds(n, tm, stride=LRU_BLOCKS), :] for n in range(LRU_BLOCKS)],
                          axis=1)
    l = _rms(lru, gl_ref[...]).astype(_BF16)
    y = jnp.dot(a, wa_ref[...], preferred_element_type=_F32)
    y = y + jnp.dot(l, wl_ref[...], preferred_element_type=_F32)
    y_ref[0] = x_ref[0] + y


def _out_proj(x, attn, lru, ga, gl, w_out_a, w_out_l, tm):
    B, S, _ = x.shape
    tok = lambda w: pl.BlockSpec((1, tm, w), lambda b, i: (b, i, 0))
    return pl.pallas_call(
        _out_proj_kernel,
        grid=(B, S // tm),
        in_specs=[tok(D_MODEL), tok(ATTN_WIDTH), _lru_rows_spec(tm),
                  _const_spec((1, ATTN_WIDTH)), _const_spec((1, LRU_WIDTH)),
                  _const_spec((ATTN_WIDTH, D_MODEL)), _const_spec((LRU_WIDTH, D_MODEL))],
        out_specs=tok(D_MODEL),
        out_shape=jax.ShapeDtypeStruct(x.shape, _F32),
        compiler_params=_params(("parallel", "parallel")),
        name="out_proj",
    )(x, attn, lru, ga, gl, w_out_a, w_out_l)


def _ffn_kernel(x_ref, g_ref, wg_ref, wu_ref, wd_ref, y_ref, h_scr):
    @pl.when(pl.program_id(2) == 0)
    def _():
        x = x_ref[0]
        h_scr[...] = _rms(x, g_ref[...]).astype(_BF16)
        y_ref[0] = x

    h = h_scr[...]
    gate = jnp.dot(h, wg_ref[...], preferred_element_type=_F32)
    up = jnp.dot(h, wu_ref[...], preferred_element_type=_F32)
    act = (jax.nn.silu(gate) * up).astype(_BF16)
    y_ref[0] += jnp.dot(act, wd_ref[...], preferred_element_type=_F32)


def _ffn(x, g, w_gate, w_up, w_down, tm, tf):
    B, S, _ = x.shape
    tok = pl.BlockSpec((1, tm, D_MODEL), lambda b, i, j: (b, i, 0))
    return pl.pallas_call(
        _ffn_kernel,
        grid=(B, S // tm, D_FF // tf),
        in_specs=[tok, pl.BlockSpec((1, D_MODEL), lambda b, i, j: (0, 0)),
                  pl.BlockSpec((D_MODEL, tf), lambda b, i, j: (0, j)),
                  pl.BlockSpec((D_MODEL, tf), lambda b, i, j: (0, j)),
                  pl.BlockSpec((tf, D_MODEL), lambda b, i, j: (j, 0))],
        out_specs=tok,
        out_shape=jax.ShapeDtypeStruct(x.shape, _F32),
        scratch_shapes=[pltpu.VMEM((tm, D_MODEL), _BF16)],
        compiler_params=_params(("parallel", "parallel", "arbitrary")),
        name="ffn",
    )(x, g, w_gate, w_up, w_down)


def _swap_halves(w):
    half = ROPE // 2
    return jnp.concatenate([-w[..., half:], w[..., :half]], axis=-1)


def _swap_gain(g):
    half = ROPE // 2
    return jnp.concatenate([g[half:], g[:half]])


def _gain_rows(g):
    pad = jnp.zeros((LANES - ROPE,), _F32)
    pe = g[NOPE:]
    return jnp.stack([g[:NOPE], jnp.concatenate([pe, pad]),
                      jnp.concatenate([_swap_gain(pe), pad])])


def _prepare(attn_norm_g, w_in, q_a_norm_g, w_uq, kv_a_norm_g, w_ukv, q_norm_g, k_norm_g,
             conv_w, conv_b, w_rg_r, b_rg_r, w_rg_i, b_rg_i, lru_lambda,
             attn_out_norm_g, lru_out_norm_g, w_out, ffn_norm_g, w_gate, w_up, w_down):
    k_pe = w_in[:, OFF_KV:OFF_KV + ROPE]
    w_in_ext = jnp.concatenate(
        [w_in[:, :OFF_KV + ROPE], _swap_halves(k_pe), w_in[:, OFF_KV + ROPE:]], axis=1).astype(_BF16)
    wq = w_uq.reshape(Q_RANK, N_HEADS, QK_DIM)
    w_uq_ext = jnp.concatenate([wq, _swap_halves(wq[..., NOPE:])], axis=-1)
    w_uq_ext = w_uq_ext.reshape(Q_RANK, N_HEADS * HEAD_PAD).astype(_BF16)
    wkv = w_ukv.reshape(KV_RANK, N_HEADS, NOPE + V_DIM)
    w_uk = wkv[..., :NOPE].reshape(KV_RANK, N_HEADS * NOPE).astype(_BF16)
    w_uvt = wkv[..., NOPE:].reshape(KV_RANK, ATTN_WIDTH).T.astype(_BF16)
    wg = (0.5 * jnp.concatenate([w_rg_r, w_rg_i], axis=-1)).astype(_BF16)
    bg = 0.5 * jnp.concatenate([b_rg_r, b_rg_i], axis=-1)[:, :, None, :]
    lam = lru_lambda.reshape(2, LRU_BLOCKS, 1, LRU_BD)
    return dict(
        g_in=attn_norm_g[None], w_in_ext=w_in_ext, g_qa=q_a_norm_g[None], w_uq_ext=w_uq_ext,
        g_kva=kv_a_norm_g[None], w_uk=w_uk, w_uvt=w_uvt,
        gq=_gain_rows(q_norm_g), gk=_gain_rows(k_norm_g),
        conv_w=conv_w.reshape(CONV_W, LRU_BLOCKS, LRU_BD), conv_b=conv_b.reshape(LRU_BLOCKS, LRU_BD),
        wg=wg, bg=bg, lam=lam,
        ga=attn_out_norm_g[None], gl=lru_out_norm_g[None],
        w_out_a=w_out[:ATTN_WIDTH].astype(_BF16), w_out_l=w_out[ATTN_WIDTH:].astype(_BF16),
        g_ffn=ffn_norm_g[None], w_gate=w_gate.astype(_BF16), w_up=w_up.astype(_BF16),
        w_down=w_down.astype(_BF16))


def _rope_tables(S):
    half = ROPE // 2
    inv_freq = ROPE_BASE ** (-2.0 * jnp.arange(half, dtype=_F32) / ROPE)
    ang = jnp.arange(S, dtype=_F32)[:, None] * inv_freq[None, :]
    pad = jnp.zeros((S, LANES - ROPE), _F32)
    cos = jnp.concatenate([jnp.cos(ang), jnp.cos(ang), pad], axis=1)
    sin = jnp.concatenate([jnp.sin(ang), jnp.sin(ang), pad], axis=1)
    return cos, sin


def _tiles(S):
    return dict(in_proj=min(256, S), attn_q=min(512, S), attn_k=min(512, S // 2),
                attn_span=max(S // 2, min(512, S)),
                lru=min(256, S), out_proj=min(512, S), ffn=min(512, S), ffn_cols=512)


def _encoder_layer(x, p):
    B, S, _ = x.shape
    t = _tiles(S)
    cos, sin = _rope_tables(S)
    q, k, vt, ux, ug = _in_proj(x, cos, sin, p["g_in"], p["w_in_ext"], p["g_qa"], p["w_uq_ext"],
                               p["g_kva"], p["w_uk"], p["w_uvt"], p["gq"], p["gk"], t["in_proj"])
    attn = _attention(q, k, vt, t["attn_q"], t["attn_k"], t["attn_span"])
    shape4 = (B, S, LRU_BLOCKS, LRU_BD)
    lru = _lru(ux.reshape(shape4), ug.reshape(shape4), p["conv_w"], p["conv_b"],
               p["wg"], p["bg"], p["lam"], t["lru"]).reshape(B, S * LRU_BLOCKS, LRU_BD)
    x1 = _out_proj(x, attn, lru, p["ga"], p["gl"], p["w_out_a"], p["w_out_l"], t["out_proj"])
    return _ffn(x1, p["g_ffn"], p["w_gate"], p["w_up"], p["w_down"], t["ffn"], t["ffn_cols"])


def kernel(x_prompt, x_sample, attn_norm_g, w_in, q_a_norm_g, w_uq, kv_a_norm_g, w_ukv, q_norm_g, k_norm_g, conv_w, conv_b, w_rg_r, b_rg_r, w_rg_i, b_rg_i, lru_lambda, attn_out_norm_g, lru_out_norm_g, w_out, ffn_norm_g, w_gate, w_up, w_down):
    depth = attn_norm_g.shape[0]
    weights = (attn_norm_g, w_in, q_a_norm_g, w_uq, kv_a_norm_g, w_ukv, q_norm_g, k_norm_g,
               conv_w, conv_b, w_rg_r, b_rg_r, w_rg_i, b_rg_i, lru_lambda,
               attn_out_norm_g, lru_out_norm_g, w_out, ffn_norm_g, w_gate, w_up, w_down)
    y_prompt, y_sample = x_prompt, x_sample
    for layer in range(depth):
        p = _prepare(*(w[layer] for w in weights))
        y_prompt = _encoder_layer(y_prompt, p)
        y_sample = _encoder_layer(y_sample, p)
    return (y_prompt, y_sample)
```

```python
import functools

import jax
import jax.numpy as jnp
import numpy as np
from jax import lax
from jax.experimental import pallas as pl
from jax.experimental.pallas import tpu as pltpu

D_MODEL = 2048
N_HEADS = 8
NOPE = 128
ROPE = 64
QK_DIM = NOPE + ROPE
V_DIM = 128
V_EXT = V_DIM + 16
ATTN_WIDTH = N_HEADS * V_DIM
Q_RANK = 512
KV_RANK = 256
ROPE_BASE = 10000.0
LRU_WIDTH = D_MODEL - ATTN_WIDTH
LRU_BLOCKS = 8
LRU_BD = LRU_WIDTH // LRU_BLOCKS
CONV_W = 4
LRU_C = 8.0
D_FF = 5632
EPS = 1e-6
LOG2_E = 1.4426950408889634

LANES = 128
SUBLANES = 8
HEAD_PAD = 2 * LANES
OFF_Q = Q_RANK
OFF_KV = OFF_Q + KV_RANK
OFF_KG = OFF_KV + LANES
OFF_UX = OFF_KG + LRU_WIDTH
IN_COLS_EXT = OFF_UX + LRU_WIDTH
VMEM_LIMIT = 56 * 1024 * 1024

_F32 = jnp.float32
_BF16 = jnp.bfloat16


def _const_spec(shape):
    return pl.BlockSpec(shape, lambda *_: (0,) * len(shape), pipeline_mode=pl.Buffered(1))


def _params(sem):
    return pltpu.CompilerParams(dimension_semantics=sem, vmem_limit_bytes=VMEM_LIMIT)


def _rms(x, g):
    return x * lax.rsqrt(jnp.mean(x * x, axis=-1, keepdims=True) + EPS) * g


def _lru_rows_spec(tm):
    return pl.BlockSpec((1, tm * LRU_BLOCKS, LRU_BD), lambda b, i: (b, i, 0))


def _in_proj_kernel(x_ref, cos_ref, sin_ref, g_in_ref, w_in_ref, g_qa_ref, w_uq_ref,
                    g_kva_ref, w_uk_ref, w_uvt_ref, gq_ref, gk_ref,
                    q_ref, k_ref, vt_ref, ux_ref, ug_ref):
    x = x_ref[0]
    h = _rms(x, g_in_ref[...]).astype(_BF16)
    u = jnp.dot(h, w_in_ref[:, :OFF_KG], preferred_element_type=_F32)

    cos = cos_ref[...]
    sin = sin_ref[...]
    lane = lax.broadcasted_iota(jnp.int32, (1, LANES), 1)
    rope_mask = (lane < ROPE).astype(_F32)
    scale = QK_DIM ** -0.5 * LOG2_E

    def rotary(grp, g_rows):
        return grp * (g_rows[1:2] * cos) + pltpu.roll(grp, ROPE, axis=1) * (g_rows[2:3] * sin)

    def head_rms(nope, grp, extra=None):
        ss = jnp.sum(nope * nope, axis=-1, keepdims=True)
        ss = ss + (jnp.sum(grp * grp * rope_mask, axis=-1, keepdims=True) if extra is None else extra)
        return lax.rsqrt(ss * (1.0 / QK_DIM) + EPS)

    cq = _rms(u[:, :OFF_Q], g_qa_ref[...]).astype(_BF16)
    qf = jnp.dot(cq, w_uq_ref[...], preferred_element_type=_F32)
    gq = gq_ref[...]
    for hd in range(N_HEADS):
        nope = qf[:, hd * HEAD_PAD:hd * HEAD_PAD + NOPE]
        grp = qf[:, hd * HEAD_PAD + NOPE:(hd + 1) * HEAD_PAD]
        r = head_rms(nope, grp) * scale
        q_ref[0, hd, :, :NOPE] = (nope * r * gq[0:1]).astype(_BF16)
        q_ref[0, hd, :, NOPE:] = (rotary(grp, gq) * r).astype(_BF16)

    ckv = _rms(u[:, OFF_Q:OFF_KV], g_kva_ref[...]).astype(_BF16)
    vt = lax.dot_general(w_uvt_ref[...], ckv, (((1,), (1,)), ((), ())), preferred_element_type=_F32)
    ones_rows = (lax.broadcasted_iota(jnp.int32, (V_EXT - V_DIM, vt.shape[1]), 0) == 0).astype(_BF16)
    for hd in range(N_HEADS):
        vt_ref[0, hd, :V_DIM, :] = vt[hd * V_DIM:(hd + 1) * V_DIM].astype(_BF16)
        vt_ref[0, hd, V_DIM:, :] = ones_rows
    kn = jnp.dot(ckv, w_uk_ref[...], preferred_element_type=_F32)
    gk = gk_ref[...]
    kgrp = u[:, OFF_KV:OFF_KG]
    ss_pe = jnp.sum(kgrp * kgrp * rope_mask, axis=-1, keepdims=True)
    krot = rotary(kgrp, gk)
    for hd in range(N_HEADS):
        nope = kn[:, hd * NOPE:(hd + 1) * NOPE]
        r = head_rms(nope, None, ss_pe)
        k_ref[0, hd, :, :NOPE] = (nope * r * gk[0:1]).astype(_BF16)
        k_ref[0, hd, :, NOPE:] = (krot * r).astype(_BF16)

    ul = jnp.dot(h, w_in_ref[:, OFF_KG:], preferred_element_type=_F32)
    for n in range(LRU_BLOCKS):
        rows = pl.ds(n, x.shape[0], stride=LRU_BLOCKS)
        ux_ref[0, rows, :] = ul[:, n * LRU_BD:(n + 1) * LRU_BD]
        ug_ref[0, rows, :] = ul[:, LRU_WIDTH + n * LRU_BD:LRU_WIDTH + (n + 1) * LRU_BD]


def _in_proj(x, cos, sin, g_in, w_in_ext, g_qa, w_uq_ext, g_kva, w_uk, w_uvt, gq, gk, tm):
    B, S, _ = x.shape
    grid = (B, S // tm)
    tok = lambda w: pl.BlockSpec((1, tm, w), lambda b, i: (b, i, 0))
    head = pl.BlockSpec((1, N_HEADS, tm, HEAD_PAD), lambda b, i: (b, 0, i, 0))
    tab = pl.BlockSpec((tm, LANES), lambda b, i: (i, 0))
    return pl.pallas_call(
        _in_proj_kernel,
        grid=grid,
        in_specs=[tok(D_MODEL), tab, tab,
                  _const_spec((1, D_MODEL)), _const_spec((D_MODEL, IN_COLS_EXT)),
                  _const_spec((1, Q_RANK)), _const_spec((Q_RANK, N_HEADS * HEAD_PAD)),
                  _const_spec((1, KV_RANK)), _const_spec((KV_RANK, N_HEADS * NOPE)),
                  _const_spec((ATTN_WIDTH, KV_RANK)),
                  _const_spec((3, LANES)), _const_spec((3, LANES))],
        out_specs=[head, head, pl.BlockSpec((1, N_HEADS, V_EXT, tm), lambda b, i: (b, 0, 0, i)),
                   _lru_rows_spec(tm), _lru_rows_spec(tm)],
        out_shape=[jax.ShapeDtypeStruct((B, N_HEADS, S, HEAD_PAD), _BF16),
                   jax.ShapeDtypeStruct((B, N_HEADS, S, HEAD_PAD), _BF16),
                   jax.ShapeDtypeStruct((B, N_HEADS, V_EXT, S), _BF16),
                   jax.ShapeDtypeStruct((B, S * LRU_BLOCKS, LRU_BD), _F32),
                   jax.ShapeDtypeStruct((B, S * LRU_BLOCKS, LRU_BD), _F32)],
        compiler_params=_params(("parallel", "parallel")),
        name="in_proj",
    )(x, cos, sin, g_in, w_in_ext, g_qa, w_uq_ext, g_kva, w_uk, w_uvt, gq, gk)


def _attention_kernel(q_ref, k_ref, vt_ref, o_ref, s_scr, qt_scr, *, tq, tk, unroll):
    n = k_ref.shape[2] // tk
    nq = q_ref.shape[2] // tq

    def load_qt(qi, slot):
        qt_scr[slot] = q_ref[0, 0, pl.ds(pl.multiple_of(qi * tq, tq), tq), :].T

    def scores(qslot, j, slot):
        rows = pl.ds(pl.multiple_of(j * tk, tk), tk)
        s = jnp.dot(k_ref[0, 0, rows, :], qt_scr[qslot], preferred_element_type=_F32)
        s_scr[slot] = s
        return jnp.max(s, axis=0, keepdims=True)

    def update(j, slot, m, acc, cmax):
        cols = pl.ds(pl.multiple_of(j * tk, tk), tk)
        m_new = jnp.maximum(m, cmax)
        p = jnp.exp2(s_scr[slot] - m_new).astype(_BF16)
        acc = jnp.exp2(m - m_new) * acc + jnp.dot(vt_ref[0, 0, :, cols], p,
                                                  preferred_element_type=_F32)
        return m_new, acc

    def tile(qi, c0):
        qslot = qi % 2

        def body(jj, carry):
            m, acc, c0 = carry
            j = 2 * jj
            c1 = scores(qslot, j + 1, 1)
            m, acc = update(j, 0, m, acc, c0)
            c0 = scores(qslot, j + 2, 0)
            m, acc = update(j + 1, 1, m, acc, c1)
            return m, acc, c0

        carry = (jnp.full((1, tq), -jnp.inf, _F32), jnp.zeros((V_EXT, tq), _F32), c0)
        m, acc, c0 = lax.fori_loop(0, n // 2 - 1, body, carry, unroll=unroll)
        c1 = scores(qslot, n - 1, 1)
        m, acc = update(n - 2, 0, m, acc, c0)
        load_qt(jnp.minimum(qi + 1, nq - 1), 1 - qslot)
        c_next = scores(1 - qslot, 0, 0)
        m, acc = update(n - 1, 1, m, acc, c1)
        out = acc[:V_DIM] / acc[V_DIM:V_DIM + 1]
        o_ref[0, pl.ds(pl.multiple_of(qi * tq, tq), tq), :] = out.T.astype(o_ref.dtype)
        return c_next

    load_qt(0, 0)
    lax.fori_loop(0, nq, tile, scores(0, 0, 0))


def _attention(q, k, vt, tq, tk, span):
    B, H, S, _ = q.shape
    pairs = S // tk // 2 - 1
    assert S % (2 * tk) == 0 and pairs >= 0 and S % span == 0 and span % tq == 0
    unroll = max(u for u in range(1, 6) if pairs % u == 0) if pairs else 1
    return pl.pallas_call(
        functools.partial(_attention_kernel, tq=tq, tk=tk, unroll=unroll),
        grid=(B, H, S // span),
        in_specs=[pl.BlockSpec((1, 1, span, HEAD_PAD), lambda b, h, i: (b, h, i, 0)),
                  pl.BlockSpec((1, 1, S, HEAD_PAD), lambda b, h, i: (b, h, 0, 0)),
                  pl.BlockSpec((1, 1, V_EXT, S), lambda b, h, i: (b, h, 0, 0))],
        out_specs=pl.BlockSpec((1, span, V_DIM), lambda b, h, i: (b, i, h)),
        out_shape=jax.ShapeDtypeStruct((B, S, ATTN_WIDTH), _BF16),
        scratch_shapes=[pltpu.VMEM((2, tk, tq), _F32), pltpu.VMEM((2, HEAD_PAD, tq), _BF16)],
        compiler_params=_params(("parallel", "parallel", "arbitrary")),
        name="attention",
    )(q, k, vt)


def _lru_gates(ux_ref, prev_ref, next_ref, cw_ref, cb_ref, wg_ref, bg_ref, lam_ref,
               xc_scr, a_scr, b_scr, *, tt, first, last):
    x = ux_ref[0]
    prev = jnp.where(first, 0.0, prev_ref[0])
    nxt = jnp.where(last, 0.0, next_ref[0])
    xe = jnp.concatenate([prev[SUBLANES - 1:], x, nxt[:2]], axis=0)
    cw = cw_ref[...]
    xc = cb_ref[...][None]
    for tap in range(CONV_W):
        xc = xc + xe[tap:tap + tt] * cw[tap][None]
    xc_scr[...] = xc.reshape(tt * LRU_BLOCKS, LRU_BD)
    k1 = LRU_C * jax.nn.softplus(-lam_ref[...])
    k2 = -LOG2_E * k1
    for n in range(LRU_BLOCKS):
        rows = pl.ds(n, tt, stride=LRU_BLOCKS)
        xn = xc_scr[rows, :]
        z = jnp.dot(xn.astype(_BF16), wg_ref[n], preferred_element_type=_F32) + bg_ref[n]
        sig = 0.5 * jnp.tanh(z) + 0.5
        r = sig[:, :LRU_BD]
        i = sig[:, LRU_BD:]
        a = jnp.exp2(r * k2[n])
        a_scr[rows, :] = a
        y = jnp.tanh(r * k1[n]) * (1.0 + a * a)
        b_scr[rows, :] = jnp.where(y > 0.0, y * lax.rsqrt(y), 0.0) * (i * xn)


SCAN_BLOCK = 8


def _scan_block(k, h, *, a_scr, b_scr, reverse, emit):
    rows = pl.ds(pl.multiple_of(k * SCAN_BLOCK * LRU_BLOCKS, SCAN_BLOCK * LRU_BLOCKS), SCAN_BLOCK * LRU_BLOCKS)
    a = a_scr[rows, :].reshape(SCAN_BLOCK, LRU_BLOCKS, LRU_BD)
    b = b_scr[rows, :].reshape(SCAN_BLOCK, LRU_BLOCKS, LRU_BD)
    order = range(SCAN_BLOCK - 1, -1, -1) if reverse else range(SCAN_BLOCK)
    hs = [None] * SCAN_BLOCK
    for t0, t1 in zip(order[0::2], order[1::2]):
        hs[t0] = a[t0] * h + b[t0]
        h = (a[t1] * a[t0]) * h + (a[t1] * b[t0] + b[t1])
        hs[t1] = h
    emit(k, jnp.stack(hs))
    return h


def _lru_fwd_kernel(ux_ref, prev_ref, next_ref, cw_ref, cb_ref, wg_ref, bg_ref, lam_ref,
                    hf_ref, xc_scr, a_scr, b_scr, h_scr, *, tt):
    i = pl.program_id(1)
    _lru_gates(ux_ref, prev_ref, next_ref, cw_ref, cb_ref, wg_ref, bg_ref, lam_ref,
               xc_scr, a_scr, b_scr, tt=tt, first=i == 0, last=i == pl.num_programs(1) - 1)

    @pl.when(i == 0)
    def _():
        h_scr[...] = jnp.zeros(h_scr.shape, _F32)

    def emit(k, hs):
        hf_ref[0, pl.ds(k * SCAN_BLOCK, SCAN_BLOCK)] = hs

    h_scr[...] = lax.fori_loop(0, tt // SCAN_BLOCK, lambda k, h: _scan_block(
        k, h, a_scr=a_scr, b_scr=b_scr, reverse=False, emit=emit), h_scr[...])


def _lru_bwd_kernel(ux_ref, prev_ref, next_ref, cw_ref, cb_ref, wg_ref, bg_ref, lam_ref,
                    hf_ref, ug_ref, out_ref, xc_scr, a_scr, b_scr, h_scr, g_scr, *, tt):
    i = pl.program_id(1)
    _lru_gates(ux_ref, prev_ref, next_ref, cw_ref, cb_ref, wg_ref, bg_ref, lam_ref,
               xc_scr, a_scr, b_scr, tt=tt, first=i == pl.num_programs(1) - 1, last=i == 0)

    @pl.when(i == 0)
    def _():
        h_scr[...] = jnp.zeros(h_scr.shape, _F32)

    g_scr[...] = jax.nn.gelu(ug_ref[0])

    def emit(k, hs):
        toks = pl.ds(k * SCAN_BLOCK, SCAN_BLOCK)
        out_ref[0, toks] = (hf_ref[0, toks] + hs) * g_scr[toks]

    nb = tt // SCAN_BLOCK
    h_scr[...] = lax.fori_loop(0, nb, lambda i, h: _scan_block(
        nb - 1 - i, h, a_scr=a_scr, b_scr=b_scr, reverse=True, emit=emit), h_scr[...])


def _lru(ux, ug, conv_w, conv_b, wg, bg, lam, tt):
    B, S = ux.shape[:2]
    nt = S // tt
    nh = tt // SUBLANES
    tile4 = (1, tt, LRU_BLOCKS, LRU_BD)
    halo4 = (1, SUBLANES, LRU_BLOCKS, LRU_BD)
    last_halo = S // SUBLANES - 1
    scratch = [pltpu.VMEM((tt * LRU_BLOCKS, LRU_BD), _F32) for _ in range(3)]
    scratch.append(pltpu.VMEM((LRU_BLOCKS, LRU_BD), _F32))
    consts = [_const_spec((CONV_W, LRU_BLOCKS, LRU_BD)), _const_spec((LRU_BLOCKS, LRU_BD)),
              _const_spec((LRU_BLOCKS, LRU_BD, 2 * LRU_BD)),
              _const_spec((LRU_BLOCKS, 1, 2 * LRU_BD)), _const_spec((LRU_BLOCKS, 1, LRU_BD))]

    def specs(tile_of):
        return [pl.BlockSpec(tile4, lambda b, i: (b, tile_of(i), 0, 0)),
                pl.BlockSpec(halo4, lambda b, i: (b, jnp.maximum(tile_of(i) * nh - 1, 0), 0, 0)),
                pl.BlockSpec(halo4, lambda b, i: (b, jnp.minimum((tile_of(i) + 1) * nh, last_halo), 0, 0))]

    fwd_tile = lambda i: i
    bwd_tile = lambda i: nt - 1 - i
    shape4 = jax.ShapeDtypeStruct((B, S, LRU_BLOCKS, LRU_BD), _F32)
    hf = pl.pallas_call(
        functools.partial(_lru_fwd_kernel, tt=tt),
        grid=(B, nt),
        in_specs=specs(fwd_tile) + consts,
        out_specs=pl.BlockSpec(tile4, lambda b, i: (b, i, 0, 0)),
        out_shape=shape4,
        scratch_shapes=scratch,
        compiler_params=_params(("parallel", "arbitrary")),
        name="lru_forward",
    )(ux, ux, ux, conv_w, conv_b, wg[0], bg[0], lam[0])
    bwd_spec = pl.BlockSpec(tile4, lambda b, i: (b, nt - 1 - i, 0, 0))
    return pl.pallas_call(
        functools.partial(_lru_bwd_kernel, tt=tt),
        grid=(B, nt),
        in_specs=specs(bwd_tile) + consts + [bwd_spec, bwd_spec],
        out_specs=bwd_spec,
        out_shape=shape4,
        scratch_shapes=scratch + [pltpu.VMEM((tt, LRU_BLOCKS, LRU_BD), _F32)],
        compiler_params=_params(("parallel", "arbitrary")),
        name="lru_backward",
    )(ux, ux, ux, conv_w, conv_b, wg[1], bg[1], lam[1], hf, ug)


def _out_proj_kernel(x_ref, attn_ref, lru_ref, ga_ref, gl_ref, wa_ref, wl_ref, y_ref):
    a = _rms(attn_ref[0].astype(_F32), ga_ref[...]).astype(_BF16)
    tm = x_ref.shape[1]
    lru = jnp.concatenate([lru_ref[0, pl.ds(n, tm, stride=LRU_BLOCKS), :] for n in range(LRU_BLOCKS)],
                          axis=1)
    l = _rms(lru, gl_ref[...]).astype(_BF16)
    y = jnp.dot(a, wa_ref[...], preferred_element_type=_F32)
    y = y + jnp.dot(l, wl_ref[...], preferred_element_type=_F32)
    y_ref[0] = x_ref[0] + y


def _out_proj(x, attn, lru, ga, gl, w_out_a, w_out_l, tm):
    B, S, _ = x.shape
    tok = lambda w: pl.BlockSpec((1, tm, w), lambda b, i: (b, i, 0))
    return pl.pallas_call(
        _out_proj_kernel,
        grid=(B, S // tm),
        in_specs=[tok(D_MODEL), tok(ATTN_WIDTH), _lru_rows_spec(tm),
                  _const_spec((1, ATTN_WIDTH)), _const_spec((1, LRU_WIDTH)),
                  _const_spec((ATTN_WIDTH, D_MODEL)), _const_spec((LRU_WIDTH, D_MODEL))],
        out_specs=tok(D_MODEL),
        out_shape=jax.ShapeDtypeStruct(x.shape, _F32),
        compiler_params=_params(("parallel", "parallel")),
        name="out_proj",
    )(x, attn, lru, ga, gl, w_out_a, w_out_l)


def _ffn_kernel(x_ref, g_ref, wg_ref, wu_ref, wd_ref, y_ref, h_scr):
    @pl.when(pl.program_id(2) == 0)
    def _():
        x = x_ref[0]
        h_scr[...] = _rms(x, g_ref[...]).astype(_BF16)
        y_ref[0] = x

    h = h_scr[...]
    gate = jnp.dot(h, wg_ref[...], preferred_element_type=_F32)
    up = jnp.dot(h, wu_ref[...], preferred_element_type=_F32)
    act = (jax.nn.silu(gate) * up).astype(_BF16)
    y_ref[0] += jnp.dot(act, wd_ref[...], preferred_element_type=_F32)


def _ffn(x, g, w_gate, w_up, w_down, tm, tf):
    B, S, _ = x.shape
    tok = pl.BlockSpec((1, tm, D_MODEL), lambda b, i, j: (b, i, 0))
    return pl.pallas_call(
        _ffn_kernel,
        grid=(B, S // tm, D_FF // tf),
        in_specs=[tok, pl.BlockSpec((1, D_MODEL), lambda b, i, j: (0, 0)),
                  pl.BlockSpec((D_MODEL, tf), lambda b, i, j: (0, j)),
                  pl.BlockSpec((D_MODEL, tf), lambda b, i, j: (0, j)),
                  pl.BlockSpec((tf, D_MODEL), lambda b, i, j: (j, 0))],
        out_specs=tok,
        out_shape=jax.ShapeDtypeStruct(x.shape, _F32),
        scratch_shapes=[pltpu.VMEM((tm, D_MODEL), _BF16)],
        compiler_params=_params(("parallel", "parallel", "arbitrary")),
        name="ffn",
    )(x, g, w_gate, w_up, w_down)


def _swap_halves(w):
    half = ROPE // 2
    return jnp.concatenate([-w[..., half:], w[..., :half]], axis=-1)


def _swap_gain(g):
    half = ROPE // 2
    return jnp.concatenate([g[half:], g[:half]])


def _gain_rows(g):
    pad = jnp.zeros((LANES - ROPE,), _F32)
    pe = g[NOPE:]
    return jnp.stack([g[:NOPE], jnp.concatenate([pe, pad]),
                      jnp.concatenate([_swap_gain(pe), pad])])


def _prepare(attn_norm_g, w_in, q_a_norm_g, w_uq, kv_a_norm_g, w_ukv, q_norm_g, k_norm_g,
             conv_w, conv_b, w_rg_r, b_rg_r, w_rg_i, b_rg_i, lru_lambda,
             attn_out_norm_g, lru_out_norm_g, w_out, ffn_norm_g, w_gate, w_up, w_down):
    k_pe = w_in[:, OFF_KV:OFF_KV + ROPE]
    w_in_ext = jnp.concatenate(
        [w_in[:, :OFF_KV + ROPE], _swap_halves(k_pe), w_in[:, OFF_KV + ROPE:]], axis=1).astype(_BF16)
    wq = w_uq.reshape(Q_RANK, N_HEADS, QK_DIM)
    w_uq_ext = jnp.concatenate([wq, _swap_halves(wq[..., NOPE:])], axis=-1)
    w_uq_ext = w_uq_ext.reshape(Q_RANK, N_HEADS * HEAD_PAD).astype(_BF16)
    wkv = w_ukv.reshape(KV_RANK, N_HEADS, NOPE + V_DIM)
    w_uk = wkv[..., :NOPE].reshape(KV_RANK, N_HEADS * NOPE).astype(_BF16)
    w_uvt = wkv[..., NOPE:].reshape(KV_RANK, ATTN_WIDTH).T.astype(_BF16)
    wg = (0.5 * jnp.concatenate([w_rg_r, w_rg_i], axis=-1)).astype(_BF16)
    bg = 0.5 * jnp.concatenate([b_rg_r, b_rg_i], axis=-1)[:, :, None, :]
    lam = lru_lambda.reshape(2, LRU_BLOCKS, 1, LRU_BD)
    return dict(
        g_in=attn_norm_g[None], w_in_ext=w_in_ext, g_qa=q_a_norm_g[None], w_uq_ext=w_uq_ext,
        g_kva=kv_a_norm_g[None], w_uk=w_uk, w_uvt=w_uvt,
        gq=_gain_rows(q_norm_g), gk=_gain_rows(k_norm_g),
        conv_w=conv_w.reshape(CONV_W, LRU_BLOCKS, LRU_BD), conv_b=conv_b.reshape(LRU_BLOCKS, LRU_BD),
        wg=wg, bg=bg, lam=lam,
        ga=attn_out_norm_g[None], gl=lru_out_norm_g[None],
        w_out_a=w_out[:ATTN_WIDTH].astype(_BF16), w_out_l=w_out[ATTN_WIDTH:].astype(_BF16),
        g_ffn=ffn_norm_g[None], w_gate=w_gate.astype(_BF16), w_up=w_up.astype(_BF16),
        w_down=w_down.astype(_BF16))


def _rope_tables(S):
    half = ROPE // 2
    inv_freq = ROPE_BASE ** (-2.0 * np.arange(half, dtype=np.float64) / ROPE)
    ang = np.arange(S, dtype=np.float64)[:, None] * inv_freq[None, :]
    pad = np.zeros((S, LANES - ROPE))
    cos = np.concatenate([np.cos(ang), np.cos(ang), pad], axis=1)
    sin = np.concatenate([np.sin(ang), np.sin(ang), pad], axis=1)
    return jnp.asarray(cos, _F32), jnp.asarray(sin, _F32)


def _tiles(S):
    return dict(in_proj=min(512, S), attn_q=min(512, S), attn_k=min(512, S // 2),
                attn_span=max(S // 2, min(512, S)),
                lru=min(512, S), out_proj=min(512, S), ffn=min(512, S), ffn_cols=512)


def _encoder_layer(x, p):
    B, S, _ = x.shape
    t = _tiles(S)
    cos, sin = _rope_tables(S)
    q, k, vt, ux, ug = _in_proj(x, cos, sin, p["g_in"], p["w_in_ext"], p["g_qa"], p["w_uq_ext"],
                               p["g_kva"], p["w_uk"], p["w_uvt"], p["gq"], p["gk"], t["in_proj"])
    attn = _attention(q, k, vt, t["attn_q"], t["attn_k"], t["attn_span"])
    shape4 = (B, S, LRU_BLOCKS, LRU_BD)
    lru = _lru(ux.reshape(shape4), ug.reshape(shape4), p["conv_w"], p["conv_b"],
               p["wg"], p["bg"], p["lam"], t["lru"]).reshape(B, S * LRU_BLOCKS, LRU_BD)
    x1 = _out_proj(x, attn, lru, p["ga"], p["gl"], p["w_out_a"], p["w_out_l"], t["out_proj"])
    return _ffn(x1, p["g_ffn"], p["w_gate"], p["w_up"], p["w_down"], t["ffn"], t["ffn_cols"])


def kernel(x_prompt, x_sample, attn_norm_g, w_in, q_a_norm_g, w_uq, kv_a_norm_g, w_ukv, q_norm_g, k_norm_g, conv_w, conv_b, w_rg_r, b_rg_r, w_rg_i, b_rg_i, lru_lambda, attn_out_norm_g, lru_out_norm_g, w_out, ffn_norm_g, w_gate, w_up, w_down):
    depth = attn_norm_g.shape[0]
    weights = (attn_norm_g, w_in, q_a_norm_g, w_uq, kv_a_norm_g, w_ukv, q_norm_g, k_norm_g,
               conv_w, conv_b, w_rg_r, b_rg_r, w_rg_i, b_rg_i, lru_lambda,
               attn_out_norm_g, lru_out_norm_g, w_out, ffn_norm_g, w_gate, w_up, w_down)
    y_prompt, y_sample = x_prompt, x_sample
    for layer in range(depth):
        p = _prepare(*(w[layer] for w in weights))
        y_prompt = _encoder_layer(y_prompt, p)
        y_sample = _encoder_layer(y_sample, p)
    return (y_prompt, y_sample)
```

```python
import functools

import jax
import jax.numpy as jnp
import numpy as np
from jax import lax
from jax.experimental import pallas as pl
from jax.experimental.pallas import tpu as pltpu

D_MODEL = 2048
N_HEADS = 8
NOPE = 128
ROPE = 64
QK_DIM = NOPE + ROPE
V_DIM = 128
BF16_SUBLANES = 16
V_EXT = V_DIM + BF16_SUBLANES
ATTN_WIDTH = N_HEADS * V_DIM
Q_RANK = 512
KV_RANK = 256
ROPE_BASE = 10000.0
LRU_WIDTH = D_MODEL - ATTN_WIDTH
LRU_BLOCKS = 8
LRU_BD = LRU_WIDTH // LRU_BLOCKS
CONV_W = 4
LRU_C = 8.0
D_FF = 5632
EPS = 1e-6
LOG2_E = 1.4426950408889634

LANES = 128
SUBLANES = 8
HEAD_PAD = 2 * LANES
OFF_Q = Q_RANK
OFF_KV = OFF_Q + KV_RANK
OFF_KG = OFF_KV + LANES
VMEM_LIMIT = 56 * 1024 * 1024

_F32 = jnp.float32
_BF16 = jnp.bfloat16


def _const_spec(shape):
    return pl.BlockSpec(shape, lambda *_: (0,) * len(shape), pipeline_mode=pl.Buffered(1))


def _params(sem):
    return pltpu.CompilerParams(dimension_semantics=sem, vmem_limit_bytes=VMEM_LIMIT)


def _rms(x, g):
    return x * lax.rsqrt(jnp.mean(x * x, axis=-1, keepdims=True) + EPS) * g


def _lru_rows_spec(tm):
    return pl.BlockSpec((1, tm * LRU_BLOCKS, LRU_BD), lambda b, i: (b, i, 0))


def _in_proj_kernel(x_ref, cos_ref, sin_ref, g_in_ref, w_head_ref, w_lru_ref, g_qa_ref, w_uq_ref,
                    g_kva_ref, w_uk_ref, w_uvt_ref, gq_ref, gk_ref,
                    q_ref, k_ref, vt_ref, ux_ref, ug_ref):
    x = x_ref[0]
    h = _rms(x, g_in_ref[...]).astype(_BF16)
    u = jnp.dot(h, w_head_ref[...], preferred_element_type=_F32)

    cos = cos_ref[...]
    sin = sin_ref[...]
    lane = lax.broadcasted_iota(jnp.int32, (1, LANES), 1)
    rope_mask = (lane < ROPE).astype(_F32)
    scale = QK_DIM ** -0.5 * LOG2_E

    def rotary(grp, g_rows):
        return grp * (g_rows[1:2] * cos) + pltpu.roll(grp, ROPE, axis=1) * (g_rows[2:3] * sin)

    def head_rms(nope, grp, extra=None):
        ss = jnp.sum(nope * nope, axis=-1, keepdims=True)
        ss = ss + (jnp.sum(grp * grp * rope_mask, axis=-1, keepdims=True) if extra is None else extra)
        return lax.rsqrt(ss * (1.0 / QK_DIM) + EPS)

    cq = _rms(u[:, :OFF_Q], g_qa_ref[...]).astype(_BF16)
    qf = jnp.dot(cq, w_uq_ref[...], preferred_element_type=_F32)
    gq = gq_ref[...]
    for hd in range(N_HEADS):
        nope = qf[:, hd * HEAD_PAD:hd * HEAD_PAD + NOPE]
        grp = qf[:, hd * HEAD_PAD + NOPE:(hd + 1) * HEAD_PAD]
        r = head_rms(nope, grp) * scale
        q_ref[0, hd, :, :NOPE] = (nope * r * gq[0:1]).astype(_BF16)
        q_ref[0, hd, :, NOPE:] = (rotary(grp, gq) * r).astype(_BF16)

    ckv = _rms(u[:, OFF_Q:OFF_KV], g_kva_ref[...]).astype(_BF16)
    vt = lax.dot_general(w_uvt_ref[...], ckv, (((1,), (1,)), ((), ())), preferred_element_type=_F32)
    ones_rows = (lax.broadcasted_iota(jnp.int32, (V_EXT - V_DIM, vt.shape[1]), 0) == 0).astype(_BF16)
    for hd in range(N_HEADS):
        vt_ref[0, hd, :V_DIM, :] = vt[hd * V_DIM:(hd + 1) * V_DIM].astype(_BF16)
        vt_ref[0, hd, V_DIM:, :] = ones_rows
    kn = jnp.dot(ckv, w_uk_ref[...], preferred_element_type=_F32)
    gk = gk_ref[...]
    kgrp = u[:, OFF_KV:OFF_KG]
    ss_pe = jnp.sum(kgrp * kgrp * rope_mask, axis=-1, keepdims=True)
    krot = rotary(kgrp, gk)
    for hd in range(N_HEADS):
        nope = kn[:, hd * NOPE:(hd + 1) * NOPE]
        r = head_rms(nope, None, ss_pe)
        k_ref[0, hd, :, :NOPE] = (nope * r * gk[0:1]).astype(_BF16)
        k_ref[0, hd, :, NOPE:] = (krot * r).astype(_BF16)

    ul = jnp.dot(h, w_lru_ref[...], preferred_element_type=_F32)
    for n in range(LRU_BLOCKS):
        rows = pl.ds(n, x.shape[0], stride=LRU_BLOCKS)
        ux_ref[0, rows, :] = ul[:, n * LRU_BD:(n + 1) * LRU_BD]
        ug_ref[0, rows, :] = ul[:, LRU_WIDTH + n * LRU_BD:LRU_WIDTH + (n + 1) * LRU_BD]


def _in_proj(x, cos, sin, g_in, w_head, w_lru, g_qa, w_uq_ext, g_kva, w_uk, w_uvt, gq, gk, tm):
    B, S, _ = x.shape
    grid = (B, S // tm)
    tok = lambda w: pl.BlockSpec((1, tm, w), lambda b, i: (b, i, 0))
    head = pl.BlockSpec((1, N_HEADS, tm, HEAD_PAD), lambda b, i: (b, 0, i, 0))
    tab = pl.BlockSpec((tm, LANES), lambda b, i: (i, 0))
    return pl.pallas_call(
        _in_proj_kernel,
        grid=grid,
        in_specs=[tok(D_MODEL), tab, tab,
                  _const_spec((1, D_MODEL)), _const_spec((D_MODEL, OFF_KG)), _const_spec((D_MODEL, 2 * LRU_WIDTH)),
                  _const_spec((1, Q_RANK)), _const_spec((Q_RANK, N_HEADS * HEAD_PAD)),
                  _const_spec((1, KV_RANK)), _const_spec((KV_RANK, N_HEADS * NOPE)),
                  _const_spec((ATTN_WIDTH, KV_RANK)),
                  _const_spec((3, LANES)), _const_spec((3, LANES))],
        out_specs=[head, head, pl.BlockSpec((1, N_HEADS, V_EXT, tm), lambda b, i: (b, 0, 0, i)),
                   _lru_rows_spec(tm), _lru_rows_spec(tm)],
        out_shape=[jax.ShapeDtypeStruct((B, N_HEADS, S, HEAD_PAD), _BF16),
                   jax.ShapeDtypeStruct((B, N_HEADS, S, HEAD_PAD), _BF16),
                   jax.ShapeDtypeStruct((B, N_HEADS, V_EXT, S), _BF16),
                   jax.ShapeDtypeStruct((B, S * LRU_BLOCKS, LRU_BD), _F32),
                   jax.ShapeDtypeStruct((B, S * LRU_BLOCKS, LRU_BD), _F32)],
        compiler_params=_params(("parallel", "parallel")),
        name="in_proj",
    )(x, cos, sin, g_in, w_head, w_lru, g_qa, w_uq_ext, g_kva, w_uk, w_uvt, gq, gk)


def _attention_kernel(q_ref, k_ref, vt_ref, o_ref, s_scr, qt_scr, *, tq, tk, unroll):
    n = k_ref.shape[2] // tk
    nq = q_ref.shape[2] // tq

    def load_qt(qi, slot):
        qt_scr[slot] = q_ref[0, 0, pl.ds(pl.multiple_of(qi * tq, tq), tq), :].T

    def scores(qslot, j, slot):
        rows = pl.ds(pl.multiple_of(j * tk, tk), tk)
        s = jnp.dot(k_ref[0, 0, rows, :], qt_scr[qslot], preferred_element_type=_F32)
        s_scr[slot] = s
        return jnp.max(s, axis=0, keepdims=True)

    def update(j, slot, m, acc, cmax):
        cols = pl.ds(pl.multiple_of(j * tk, tk), tk)
        m_new = jnp.maximum(m, cmax)
        p = jnp.exp2(s_scr[slot] - m_new).astype(_BF16)
        acc = jnp.exp2(m - m_new) * acc + jnp.dot(vt_ref[0, 0, :, cols], p,
                                                  preferred_element_type=_F32)
        return m_new, acc

    def tile(qi, c0):
        qslot = qi % 2

        def body(jj, carry):
            m, acc, c0 = carry
            j = 2 * jj
            c1 = scores(qslot, j + 1, 1)
            m, acc = update(j, 0, m, acc, c0)
            c0 = scores(qslot, j + 2, 0)
            m, acc = update(j + 1, 1, m, acc, c1)
            return m, acc, c0

        carry = (jnp.full((1, tq), -jnp.inf, _F32), jnp.zeros((V_EXT, tq), _F32), c0)
        m, acc, c0 = lax.fori_loop(0, n // 2 - 1, body, carry, unroll=unroll)
        c1 = scores(qslot, n - 1, 1)
        m, acc = update(n - 2, 0, m, acc, c0)
        load_qt(jnp.minimum(qi + 1, nq - 1), 1 - qslot)
        c_next = scores(1 - qslot, 0, 0)
        m, acc = update(n - 1, 1, m, acc, c1)
        out = acc[:V_DIM] / acc[V_DIM:V_DIM + 1]
        o_ref[0, pl.ds(pl.multiple_of(qi * tq, tq), tq), :] = out.T.astype(o_ref.dtype)
        return c_next

    load_qt(0, 0)
    lax.fori_loop(0, nq, tile, scores(0, 0, 0))


MAX_UNROLL = 5


def _attention(q, k, vt, tq, tk, span):
    B, H, S, _ = q.shape
    pairs = S // tk // 2 - 1
    assert S % (2 * tk) == 0 and pairs >= 0 and S % span == 0 and span % tq == 0
    unroll = max(u for u in range(1, MAX_UNROLL + 1) if pairs % u == 0) if pairs else 1
    return pl.pallas_call(
        functools.partial(_attention_kernel, tq=tq, tk=tk, unroll=unroll),
        grid=(B, H, S // span),
        in_specs=[pl.BlockSpec((1, 1, span, HEAD_PAD), lambda b, h, i: (b, h, i, 0)),
                  pl.BlockSpec((1, 1, S, HEAD_PAD), lambda b, h, i: (b, h, 0, 0)),
                  pl.BlockSpec((1, 1, V_EXT, S), lambda b, h, i: (b, h, 0, 0))],
        out_specs=pl.BlockSpec((1, span, V_DIM), lambda b, h, i: (b, i, h)),
        out_shape=jax.ShapeDtypeStruct((B, S, ATTN_WIDTH), _BF16),
        scratch_shapes=[pltpu.VMEM((2, tk, tq), _F32), pltpu.VMEM((2, HEAD_PAD, tq), _BF16)],
        compiler_params=_params(("parallel", "parallel", "arbitrary")),
        name="attention",
    )(q, k, vt)


def _lru_conv(ux_ref, prev_ref, next_ref, cw_ref, cb_ref, *, tt, first, last):
    prev = jnp.where(first, 0.0, prev_ref[0])
    nxt = jnp.where(last, 0.0, next_ref[0])
    xe = jnp.concatenate([prev[SUBLANES - 1:], ux_ref[0], nxt[:2]], axis=0)
    cw = cw_ref[...]
    xc = cb_ref[...][None]
    for tap in range(CONV_W):
        xc = xc + xe[tap:tap + tt] * cw[tap][None]
    return xc


def _lru_gates(xc_scr, wg_ref, bg_ref, lam_ref, a_scr, b_scr, *, tt):
    k1 = LRU_C * jax.nn.softplus(-lam_ref[...])
    k2 = -LOG2_E * k1
    for n in range(LRU_BLOCKS):
        rows = pl.ds(n, tt, stride=LRU_BLOCKS)
        xn = xc_scr[rows, :]
        z = jnp.dot(xn.astype(_BF16), wg_ref[n], preferred_element_type=_F32) + bg_ref[n]
        sig = 0.5 * jnp.tanh(z) + 0.5
        r = sig[:, :LRU_BD]
        i = sig[:, LRU_BD:]
        a = jnp.exp2(r * k2[n])
        a_scr[rows, :] = a
        y = jnp.tanh(r * k1[n]) * (1.0 + a * a)
        b_scr[rows, :] = jnp.where(y > 0.0, y * lax.rsqrt(y), 0.0) * (i * xn)


SCAN_BLOCK = 8


def _scan_block(k, h, *, a_scr, b_scr, reverse, emit):
    rows = pl.ds(pl.multiple_of(k * SCAN_BLOCK * LRU_BLOCKS, SCAN_BLOCK * LRU_BLOCKS), SCAN_BLOCK * LRU_BLOCKS)
    a = a_scr[rows, :].reshape(SCAN_BLOCK, LRU_BLOCKS, LRU_BD)
    b = b_scr[rows, :].reshape(SCAN_BLOCK, LRU_BLOCKS, LRU_BD)
    order = range(SCAN_BLOCK - 1, -1, -1) if reverse else range(SCAN_BLOCK)
    hs = [None] * SCAN_BLOCK
    for t0, t1 in zip(order[0::2], order[1::2]):
        hs[t0] = a[t0] * h + b[t0]
        h = (a[t1] * a[t0]) * h + (a[t1] * b[t0] + b[t1])
        hs[t1] = h
    emit(k, jnp.stack(hs))
    return h


def _lru_fwd_kernel(ux_ref, prev_ref, next_ref, cw_ref, cb_ref, wg_ref, bg_ref, lam_ref,
                    hf_ref, xc_ref, xc_scr, a_scr, b_scr, h_scr, *, tt):
    i = pl.program_id(1)
    xc = _lru_conv(ux_ref, prev_ref, next_ref, cw_ref, cb_ref, tt=tt,
                   first=i == 0, last=i == pl.num_programs(1) - 1)
    xc_ref[0] = xc
    xc_scr[...] = xc.reshape(tt * LRU_BLOCKS, LRU_BD)
    _lru_gates(xc_scr, wg_ref, bg_ref, lam_ref, a_scr, b_scr, tt=tt)

    @pl.when(i == 0)
    def _():
        h_scr[...] = jnp.zeros(h_scr.shape, _F32)

    def emit(k, hs):
        hf_ref[0, pl.ds(k * SCAN_BLOCK, SCAN_BLOCK)] = hs

    h_scr[...] = lax.fori_loop(0, tt // SCAN_BLOCK, lambda k, h: _scan_block(
        k, h, a_scr=a_scr, b_scr=b_scr, reverse=False, emit=emit), h_scr[...])


def _lru_bwd_kernel(xc_ref, wg_ref, bg_ref, lam_ref, hf_ref, ug_ref,
                    out_ref, a_scr, b_scr, h_scr, g_scr, *, tt):
    _lru_gates(xc_ref.at[0], wg_ref, bg_ref, lam_ref, a_scr, b_scr, tt=tt)

    @pl.when(pl.program_id(1) == 0)
    def _():
        h_scr[...] = jnp.zeros(h_scr.shape, _F32)

    g_scr[...] = jax.nn.gelu(ug_ref[0])

    def emit(k, hs):
        toks = pl.ds(k * SCAN_BLOCK, SCAN_BLOCK)
        out_ref[0, toks] = (hf_ref[0, toks] + hs) * g_scr[toks]

    nb = tt // SCAN_BLOCK
    h_scr[...] = lax.fori_loop(0, nb, lambda i, h: _scan_block(
        nb - 1 - i, h, a_scr=a_scr, b_scr=b_scr, reverse=True, emit=emit), h_scr[...])


def _lru(ux, ug, conv_w, conv_b, wg, bg, lam, tt):
    B, S = ux.shape[:2]
    nt = S // tt
    nh = tt // SUBLANES
    tile4 = (1, tt, LRU_BLOCKS, LRU_BD)
    halo4 = (1, SUBLANES, LRU_BLOCKS, LRU_BD)
    rows3 = (1, tt * LRU_BLOCKS, LRU_BD)
    last_halo = S // SUBLANES - 1
    flat = lambda: pltpu.VMEM((tt * LRU_BLOCKS, LRU_BD), _F32)
    carry = pltpu.VMEM((LRU_BLOCKS, LRU_BD), _F32)
    conv_consts = [_const_spec((CONV_W, LRU_BLOCKS, LRU_BD)), _const_spec((LRU_BLOCKS, LRU_BD))]
    gate_consts = [_const_spec((LRU_BLOCKS, LRU_BD, 2 * LRU_BD)),
                   _const_spec((LRU_BLOCKS, 1, 2 * LRU_BD)), _const_spec((LRU_BLOCKS, 1, LRU_BD))]
    fwd_tile = pl.BlockSpec(tile4, lambda b, i: (b, i, 0, 0))
    bwd_tile = pl.BlockSpec(tile4, lambda b, i: (b, nt - 1 - i, 0, 0))
    shape4 = jax.ShapeDtypeStruct((B, S, LRU_BLOCKS, LRU_BD), _F32)
    hf, xc = pl.pallas_call(
        functools.partial(_lru_fwd_kernel, tt=tt),
        grid=(B, nt),
        in_specs=[fwd_tile,
                  pl.BlockSpec(halo4, lambda b, i: (b, jnp.maximum(i * nh - 1, 0), 0, 0)),
                  pl.BlockSpec(halo4, lambda b, i: (b, jnp.minimum((i + 1) * nh, last_halo), 0, 0))]
                 + conv_consts + gate_consts,
        out_specs=[fwd_tile, fwd_tile],
        out_shape=[shape4, shape4],
        scratch_shapes=[flat(), flat(), flat(), carry],
        compiler_params=_params(("parallel", "arbitrary")),
        name="lru_forward",
    )(ux, ux, ux, conv_w, conv_b, wg[0], bg[0], lam[0])
    return pl.pallas_call(
        functools.partial(_lru_bwd_kernel, tt=tt),
        grid=(B, nt),
        in_specs=[pl.BlockSpec(rows3, lambda b, i: (b, nt - 1 - i, 0))] + gate_consts + [bwd_tile, bwd_tile],
        out_specs=bwd_tile,
        out_shape=shape4,
        scratch_shapes=[flat(), flat(), carry, pltpu.VMEM((tt, LRU_BLOCKS, LRU_BD), _F32)],
        compiler_params=_params(("parallel", "arbitrary")),
        name="lru_backward",
    )(xc.reshape(B, S * LRU_BLOCKS, LRU_BD), wg[1], bg[1], lam[1], hf, ug)


def _out_proj_kernel(x_ref, attn_ref, lru_ref, ga_ref, gl_ref, wa_ref, wl_ref, y_ref):
    a = _rms(attn_ref[0].astype(_F32), ga_ref[...]).astype(_BF16)
    tm = x_ref.shape[1]
    lru = jnp.concatenate([lru_ref[0, pl.ds(n, tm, stride=LRU_BLOCKS), :] for n in range(LRU_BLOCKS)],
                          axis=1)
    l = _rms(lru, gl_ref[...]).astype(_BF16)
    y = jnp.dot(a, wa_ref[...], preferred_element_type=_F32)
    y = y + jnp.dot(l, wl_ref[...], preferred_element_type=_F32)
    y_ref[0] = x_ref[0] + y


def _out_proj(x, attn, lru, ga, gl, w_out_a, w_out_l, tm):
    B, S, _ = x.shape
    tok = lambda w: pl.BlockSpec((1, tm, w), lambda b, i: (b, i, 0))
    return pl.pallas_call(
        _out_proj_kernel,
        grid=(B, S // tm),
        in_specs=[tok(D_MODEL), tok(ATTN_WIDTH), _lru_rows_spec(tm),
                  _const_spec((1, ATTN_WIDTH)), _const_spec((1, LRU_WIDTH)),
                  _const_spec((ATTN_WIDTH, D_MODEL)), _const_spec((LRU_WIDTH, D_MODEL))],
        out_specs=tok(D_MODEL),
        out_shape=jax.ShapeDtypeStruct(x.shape, _F32),
        compiler_params=_params(("parallel", "parallel")),
        name="out_proj",
    )(x, attn, lru, ga, gl, w_out_a, w_out_l)


def _ffn_kernel(x_ref, g_ref, wg_ref, wu_ref, wd_ref, y_ref, h_scr):
    @pl.when(pl.program_id(2) == 0)
    def _():
        x = x_ref[0]
        h_scr[...] = _rms(x, g_ref[...]).astype(_BF16)
        y_ref[0] = x

    h = h_scr[...]
    gate = jnp.dot(h, wg_ref[...], preferred_element_type=_F32)
    up = jnp.dot(h, wu_ref[...], preferred_element_type=_F32)
    act = (jax.nn.silu(gate) * up).astype(_BF16)
    y_ref[0] += jnp.dot(act, wd_ref[...], preferred_element_type=_F32)


def _ffn(x, g, w_gate, w_up, w_down, tm, tf):
    B, S, _ = x.shape
    tok = lambda **kw: pl.BlockSpec((1, tm, D_MODEL), lambda b, i, j: (b, i, 0), **kw)
    return pl.pallas_call(
        _ffn_kernel,
        grid=(B, S // tm, D_FF // tf),
        in_specs=[tok(pipeline_mode=pl.Buffered(1)), pl.BlockSpec((1, D_MODEL), lambda b, i, j: (0, 0)),
                  pl.BlockSpec((D_MODEL, tf), lambda b, i, j: (0, j)),
                  pl.BlockSpec((D_MODEL, tf), lambda b, i, j: (0, j)),
                  pl.BlockSpec((tf, D_MODEL), lambda b, i, j: (j, 0))],
        out_specs=tok(),
        out_shape=jax.ShapeDtypeStruct(x.shape, _F32),
        scratch_shapes=[pltpu.VMEM((tm, D_MODEL), _BF16)],
        compiler_params=_params(("parallel", "parallel", "arbitrary")),
        name="ffn",
    )(x, g, w_gate, w_up, w_down)


def _swap_halves(w):
    half = ROPE // 2
    return jnp.concatenate([-w[..., half:], w[..., :half]], axis=-1)


def _swap_gain(g):
    half = ROPE // 2
    return jnp.concatenate([g[half:], g[:half]])


def _gain_rows(g):
    pad = jnp.zeros((LANES - ROPE,), _F32)
    pe = g[NOPE:]
    return jnp.stack([g[:NOPE], jnp.concatenate([pe, pad]),
                      jnp.concatenate([_swap_gain(pe), pad])])


def _prepare(attn_norm_g, w_in, q_a_norm_g, w_uq, kv_a_norm_g, w_ukv, q_norm_g, k_norm_g,
             conv_w, conv_b, w_rg_r, b_rg_r, w_rg_i, b_rg_i, lru_lambda,
             attn_out_norm_g, lru_out_norm_g, w_out, ffn_norm_g, w_gate, w_up, w_down):
    k_pe = w_in[:, OFF_KV:OFF_KV + ROPE]
    w_head = jnp.concatenate([w_in[:, :OFF_KV + ROPE], _swap_halves(k_pe)], axis=1).astype(_BF16)
    w_lru = w_in[:, OFF_KV + ROPE:].astype(_BF16)
    wq = w_uq.reshape(Q_RANK, N_HEADS, QK_DIM)
    w_uq_ext = jnp.concatenate([wq, _swap_halves(wq[..., NOPE:])], axis=-1)
    w_uq_ext = w_uq_ext.reshape(Q_RANK, N_HEADS * HEAD_PAD).astype(_BF16)
    wkv = w_ukv.reshape(KV_RANK, N_HEADS, NOPE + V_DIM)
    w_uk = wkv[..., :NOPE].reshape(KV_RANK, N_HEADS * NOPE).astype(_BF16)
    w_uvt = wkv[..., NOPE:].reshape(KV_RANK, ATTN_WIDTH).T.astype(_BF16)
    wg = (0.5 * jnp.concatenate([w_rg_r, w_rg_i], axis=-1)).astype(_BF16)
    bg = 0.5 * jnp.concatenate([b_rg_r, b_rg_i], axis=-1)[:, :, None, :]
    lam = lru_lambda.reshape(2, LRU_BLOCKS, 1, LRU_BD)
    return dict(
        g_in=attn_norm_g[None], w_head=w_head, w_lru=w_lru, g_qa=q_a_norm_g[None], w_uq_ext=w_uq_ext,
        g_kva=kv_a_norm_g[None], w_uk=w_uk, w_uvt=w_uvt,
        gq=_gain_rows(q_norm_g), gk=_gain_rows(k_norm_g),
        conv_w=conv_w.reshape(CONV_W, LRU_BLOCKS, LRU_BD), conv_b=conv_b.reshape(LRU_BLOCKS, LRU_BD),
        wg=wg, bg=bg, lam=lam,
        ga=attn_out_norm_g[None], gl=lru_out_norm_g[None],
        w_out_a=w_out[:ATTN_WIDTH].astype(_BF16), w_out_l=w_out[ATTN_WIDTH:].astype(_BF16),
        g_ffn=ffn_norm_g[None], w_gate=w_gate.astype(_BF16), w_up=w_up.astype(_BF16),
        w_down=w_down.astype(_BF16))


def _rope_tables(S):
    half = ROPE // 2
    inv_freq = ROPE_BASE ** (-2.0 * np.arange(half, dtype=np.float64) / ROPE)
    ang = np.arange(S, dtype=np.float64)[:, None] * inv_freq[None, :]
    pad = np.zeros((S, LANES - ROPE))
    cos = np.concatenate([np.cos(ang), np.cos(ang), pad], axis=1)
    sin = np.concatenate([np.sin(ang), np.sin(ang), pad], axis=1)
    return jnp.asarray(cos, _F32), jnp.asarray(sin, _F32)


def _tiles(S):
    return dict(in_proj=min(512, S), attn_q=min(512, S), attn_k=min(512, S // 2),
                attn_span=max(S // 2, min(512, S)),
                lru=min(512, S), out_proj=min(512, S), ffn=min(1024, S), ffn_cols=512)


def _encoder_layer(x, p):
    B, S, _ = x.shape
    t = _tiles(S)
    cos, sin = _rope_tables(S)
    q, k, vt, ux, ug = _in_proj(x, cos, sin, p["g_in"], p["w_head"], p["w_lru"], p["g_qa"], p["w_uq_ext"],
                               p["g_kva"], p["w_uk"], p["w_uvt"], p["gq"], p["gk"], t["in_proj"])
    attn = _attention(q, k, vt, t["attn_q"], t["attn_k"], t["attn_span"])
    shape4 = (B, S, LRU_BLOCKS, LRU_BD)
    lru = _lru(ux.reshape(shape4), ug.reshape(shape4), p["conv_w"], p["conv_b"],
               p["wg"], p["bg"], p["lam"], t["lru"]).reshape(B, S * LRU_BLOCKS, LRU_BD)
    x1 = _out_proj(x, attn, lru, p["ga"], p["gl"], p["w_out_a"], p["w_out_l"], t["out_proj"])
    return _ffn(x1, p["g_ffn"], p["w_gate"], p["w_up"], p["w_down"], t["ffn"], t["ffn_cols"])


def kernel(x_prompt, x_sample, attn_norm_g, w_in, q_a_norm_g, w_uq, kv_a_norm_g, w_ukv, q_norm_g, k_norm_g, conv_w, conv_b, w_rg_r, b_rg_r, w_rg_i, b_rg_i, lru_lambda, attn_out_norm_g, lru_out_norm_g, w_out, ffn_norm_g, w_gate, w_up, w_down):
    depth = attn_norm_g.shape[0]
    weights = (attn_norm_g, w_in, q_a_norm_g, w_uq, kv_a_norm_g, w_ukv, q_norm_g, k_norm_g,
               conv_w, conv_b, w_rg_r, b_rg_r, w_rg_i, b_rg_i, lru_lambda,
               attn_out_norm_g, lru_out_norm_g, w_out, ffn_norm_g, w_gate, w_up, w_down)
    y_prompt, y_sample = x_prompt, x_sample
    for layer in range(depth):
        p = _prepare(*(w[layer] for w in weights))
        y_prompt = _encoder_layer(y_prompt, p)
        y_sample = _encoder_layer(y_sample, p)
    return (y_prompt, y_sample)
```

```python
import functools

import jax
import jax.numpy as jnp
import numpy as np
from jax import lax
from jax.experimental import pallas as pl
from jax.experimental.pallas import tpu as pltpu

D_MODEL = 2048
N_HEADS = 8
NOPE = 128
ROPE = 64
QK_DIM = NOPE + ROPE
V_DIM = 128
BF16_SUBLANES = 16
V_EXT = V_DIM + BF16_SUBLANES
ATTN_WIDTH = N_HEADS * V_DIM
Q_RANK = 512
KV_RANK = 256
ROPE_BASE = 10000.0
LRU_WIDTH = D_MODEL - ATTN_WIDTH
LRU_BLOCKS = 8
LRU_BD = LRU_WIDTH // LRU_BLOCKS
CONV_W = 4
LRU_C = 8.0
D_FF = 5632
EPS = 1e-6
LOG2_E = 1.4426950408889634

LANES = 128
SUBLANES = 8
HEAD_PAD = 2 * LANES
OFF_Q = Q_RANK
OFF_KV = OFF_Q + KV_RANK
OFF_KG = OFF_KV + LANES
VMEM_LIMIT = 56 * 1024 * 1024

_F32 = jnp.float32
_BF16 = jnp.bfloat16


def _const_spec(shape):
    return pl.BlockSpec(shape, lambda *_: (0,) * len(shape), pipeline_mode=pl.Buffered(1))


def _params(sem):
    return pltpu.CompilerParams(dimension_semantics=sem, vmem_limit_bytes=VMEM_LIMIT)


def _rms(x, g):
    return x * lax.rsqrt(jnp.mean(x * x, axis=-1, keepdims=True) + EPS) * g


def _lru_rows_spec(tm):
    return pl.BlockSpec((1, tm * LRU_BLOCKS, LRU_BD), lambda b, i: (b, i, 0))


def _in_proj_kernel(x_ref, cos_ref, sin_ref, g_in_ref, w_head_ref, w_lru_ref, g_qa_ref, w_uq_ref,
                    g_kva_ref, w_uk_ref, w_uvt_ref, gq_ref, gk_ref,
                    q_ref, k_ref, vt_ref, ux_ref, ug_ref):
    x = x_ref[0]
    h = _rms(x, g_in_ref[...]).astype(_BF16)
    u = jnp.dot(h, w_head_ref[...], preferred_element_type=_F32)

    cos = cos_ref[...]
    sin = sin_ref[...]
    lane = lax.broadcasted_iota(jnp.int32, (1, LANES), 1)
    rope_mask = (lane < ROPE).astype(_F32)
    scale = QK_DIM ** -0.5 * LOG2_E

    def rotary(grp, g_rows):
        return grp * (g_rows[1:2] * cos) + pltpu.roll(grp, ROPE, axis=1) * (g_rows[2:3] * sin)

    def head_rms(nope, grp, extra=None):
        ss = jnp.sum(nope * nope, axis=-1, keepdims=True)
        ss = ss + (jnp.sum(grp * grp * rope_mask, axis=-1, keepdims=True) if extra is None else extra)
        return lax.rsqrt(ss * (1.0 / QK_DIM) + EPS)

    cq = _rms(u[:, :OFF_Q], g_qa_ref[...]).astype(_BF16)
    qf = jnp.dot(cq, w_uq_ref[...], preferred_element_type=_F32)
    gq = gq_ref[...]
    for hd in range(N_HEADS):
        nope = qf[:, hd * HEAD_PAD:hd * HEAD_PAD + NOPE]
        grp = qf[:, hd * HEAD_PAD + NOPE:(hd + 1) * HEAD_PAD]
        r = head_rms(nope, grp) * scale
        q_ref[0, hd, :, :NOPE] = (nope * r * gq[0:1]).astype(_BF16)
        q_ref[0, hd, :, NOPE:] = (rotary(grp, gq) * r).astype(_BF16)

    ckv = _rms(u[:, OFF_Q:OFF_KV], g_kva_ref[...]).astype(_BF16)
    vt = lax.dot_general(w_uvt_ref[...], ckv, (((1,), (1,)), ((), ())), preferred_element_type=_F32)
    ones_rows = (lax.broadcasted_iota(jnp.int32, (V_EXT - V_DIM, vt.shape[1]), 0) == 0).astype(_BF16)
    for hd in range(N_HEADS):
        vt_ref[0, hd, :V_DIM, :] = vt[hd * V_DIM:(hd + 1) * V_DIM].astype(_BF16)
        vt_ref[0, hd, V_DIM:, :] = ones_rows
    kn = jnp.dot(ckv, w_uk_ref[...], preferred_element_type=_F32)
    gk = gk_ref[...]
    kgrp = u[:, OFF_KV:OFF_KG]
    ss_pe = jnp.sum(kgrp * kgrp * rope_mask, axis=-1, keepdims=True)
    krot = rotary(kgrp, gk)
    for hd in range(N_HEADS):
        nope = kn[:, hd * NOPE:(hd + 1) * NOPE]
        r = head_rms(nope, None, ss_pe)
        k_ref[0, hd, :, :NOPE] = (nope * r * gk[0:1]).astype(_BF16)
        k_ref[0, hd, :, NOPE:] = (krot * r).astype(_BF16)

    ul = jnp.dot(h, w_lru_ref[...], preferred_element_type=_F32)
    for n in range(LRU_BLOCKS):
        rows = pl.ds(n, x.shape[0], stride=LRU_BLOCKS)
        ux_ref[0, rows, :] = ul[:, n * LRU_BD:(n + 1) * LRU_BD]
        ug_ref[0, rows, :] = ul[:, LRU_WIDTH + n * LRU_BD:LRU_WIDTH + (n + 1) * LRU_BD]


def _in_proj(x, cos, sin, g_in, w_head, w_lru, g_qa, w_uq_ext, g_kva, w_uk, w_uvt, gq, gk, tm):
    B, S, _ = x.shape
    grid = (B, S // tm)
    tok = lambda w: pl.BlockSpec((1, tm, w), lambda b, i: (b, i, 0))
    head = pl.BlockSpec((1, N_HEADS, tm, HEAD_PAD), lambda b, i: (b, 0, i, 0))
    tab = pl.BlockSpec((tm, LANES), lambda b, i: (i, 0))
    return pl.pallas_call(
        _in_proj_kernel,
        grid=grid,
        in_specs=[tok(D_MODEL), tab, tab,
                  _const_spec((1, D_MODEL)), _const_spec((D_MODEL, OFF_KG)), _const_spec((D_MODEL, 2 * LRU_WIDTH)),
                  _const_spec((1, Q_RANK)), _const_spec((Q_RANK, N_HEADS * HEAD_PAD)),
                  _const_spec((1, KV_RANK)), _const_spec((KV_RANK, N_HEADS * NOPE)),
                  _const_spec((ATTN_WIDTH, KV_RANK)),
                  _const_spec((3, LANES)), _const_spec((3, LANES))],
        out_specs=[head, head, pl.BlockSpec((1, N_HEADS, V_EXT, tm), lambda b, i: (b, 0, 0, i)),
                   _lru_rows_spec(tm), _lru_rows_spec(tm)],
        out_shape=[jax.ShapeDtypeStruct((B, N_HEADS, S, HEAD_PAD), _BF16),
                   jax.ShapeDtypeStruct((B, N_HEADS, S, HEAD_PAD), _BF16),
                   jax.ShapeDtypeStruct((B, N_HEADS, V_EXT, S), _BF16),
                   jax.ShapeDtypeStruct((B, S * LRU_BLOCKS, LRU_BD), _F32),
                   jax.ShapeDtypeStruct((B, S * LRU_BLOCKS, LRU_BD), _F32)],
        compiler_params=_params(("parallel", "parallel")),
        name="in_proj",
    )(x, cos, sin, g_in, w_head, w_lru, g_qa, w_uq_ext, g_kva, w_uk, w_uvt, gq, gk)


def _attention_kernel(q_ref, k_ref, vt_ref, o_ref, s_scr, qt_scr, *, tq, tk, unroll):
    n = k_ref.shape[2] // tk
    nq = q_ref.shape[2] // tq

    def load_qt(qi, slot):
        qt_scr[slot] = q_ref[0, 0, pl.ds(pl.multiple_of(qi * tq, tq), tq), :].T

    def scores(qslot, j, slot):
        rows = pl.ds(pl.multiple_of(j * tk, tk), tk)
        s = jnp.dot(k_ref[0, 0, rows, :], qt_scr[qslot], preferred_element_type=_F32)
        s_scr[slot] = s
        return jnp.max(s, axis=0, keepdims=True)

    def update(j, slot, m, acc, cmax):
        cols = pl.ds(pl.multiple_of(j * tk, tk), tk)
        m_new = jnp.maximum(m, cmax)
        p = jnp.exp2(s_scr[slot] - m_new).astype(_BF16)
        acc = jnp.exp2(m - m_new) * acc + jnp.dot(vt_ref[0, 0, :, cols], p,
                                                  preferred_element_type=_F32)
        return m_new, acc

    def tile(qi, c0):
        qslot = qi % 2

        def body(jj, carry):
            m, acc, c0 = carry
            j = 2 * jj
            c1 = scores(qslot, j + 1, 1)
            m, acc = update(j, 0, m, acc, c0)
            c0 = scores(qslot, j + 2, 0)
            m, acc = update(j + 1, 1, m, acc, c1)
            return m, acc, c0

        carry = (jnp.full((1, tq), -jnp.inf, _F32), jnp.zeros((V_EXT, tq), _F32), c0)
        m, acc, c0 = lax.fori_loop(0, n // 2 - 1, body, carry, unroll=unroll)
        c1 = scores(qslot, n - 1, 1)
        m, acc = update(n - 2, 0, m, acc, c0)
        load_qt(jnp.minimum(qi + 1, nq - 1), 1 - qslot)
        c_next = scores(1 - qslot, 0, 0)
        m, acc = update(n - 1, 1, m, acc, c1)
        out = acc[:V_DIM] / acc[V_DIM:V_DIM + 1]
        o_ref[0, pl.ds(pl.multiple_of(qi * tq, tq), tq), :] = out.T.astype(o_ref.dtype)
        return c_next

    load_qt(0, 0)
    lax.fori_loop(0, nq, tile, scores(0, 0, 0))


MAX_UNROLL = 5
ATTN_SPAN = 8192


def _attention(q, k, vt, tq, tk, span):
    B, H, S, _ = q.shape
    pairs = S // tk // 2 - 1
    assert S % (2 * tk) == 0 and pairs >= 0 and S % span == 0 and span % tq == 0
    unroll = max(u for u in range(1, MAX_UNROLL + 1) if pairs % u == 0) if pairs else 1
    return pl.pallas_call(
        functools.partial(_attention_kernel, tq=tq, tk=tk, unroll=unroll),
        grid=(B, H, S // span),
        in_specs=[pl.BlockSpec((1, 1, span, HEAD_PAD), lambda b, h, i: (b, h, i, 0)),
                  pl.BlockSpec((1, 1, S, HEAD_PAD), lambda b, h, i: (b, h, 0, 0)),
                  pl.BlockSpec((1, 1, V_EXT, S), lambda b, h, i: (b, h, 0, 0))],
        out_specs=pl.BlockSpec((1, span, V_DIM), lambda b, h, i: (b, i, h)),
        out_shape=jax.ShapeDtypeStruct((B, S, ATTN_WIDTH), _BF16),
        scratch_shapes=[pltpu.VMEM((2, tk, tq), _F32), pltpu.VMEM((2, HEAD_PAD, tq), _BF16)],
        compiler_params=_params(("parallel", "parallel", "arbitrary")),
        name="attention",
    )(q, k, vt)


def _lru_conv(ux_ref, prev_ref, next_ref, cw_ref, cb_ref, *, tt, first, last):
    prev = jnp.where(first, 0.0, prev_ref[0])
    nxt = jnp.where(last, 0.0, next_ref[0])
    xe = jnp.concatenate([prev[SUBLANES - 1:], ux_ref[0], nxt[:2]], axis=0)
    cw = cw_ref[...]
    xc = cb_ref[...][None]
    for tap in range(CONV_W):
        xc = xc + xe[tap:tap + tt] * cw[tap][None]
    return xc


def _lru_gates(xc_scr, wg_ref, bg_ref, lam_ref, a_scr, b_scr, *, tt):
    k1 = LRU_C * jax.nn.softplus(-lam_ref[...])
    k2 = -LOG2_E * k1
    for n in range(LRU_BLOCKS):
        rows = pl.ds(n, tt, stride=LRU_BLOCKS)
        xn = xc_scr[rows, :]
        z = jnp.dot(xn.astype(_BF16), wg_ref[n], preferred_element_type=_F32) + bg_ref[n]
        sig = 0.5 * jnp.tanh(z) + 0.5
        r = sig[:, :LRU_BD]
        i = sig[:, LRU_BD:]
        a = jnp.exp2(r * k2[n])
        a_scr[rows, :] = a
        y = jnp.tanh(r * k1[n]) * (1.0 + a * a)
        b_scr[rows, :] = jnp.where(y > 0.0, y * lax.rsqrt(y), 0.0) * (i * xn)


SCAN_BLOCK = 8


def _scan_block(k, h, *, a_scr, b_scr, reverse, emit):
    rows = pl.ds(pl.multiple_of(k * SCAN_BLOCK * LRU_BLOCKS, SCAN_BLOCK * LRU_BLOCKS), SCAN_BLOCK * LRU_BLOCKS)
    a = a_scr[rows, :].reshape(SCAN_BLOCK, LRU_BLOCKS, LRU_BD)
    b = b_scr[rows, :].reshape(SCAN_BLOCK, LRU_BLOCKS, LRU_BD)
    order = range(SCAN_BLOCK - 1, -1, -1) if reverse else range(SCAN_BLOCK)
    hs = [None] * SCAN_BLOCK
    for t0, t1 in zip(order[0::2], order[1::2]):
        hs[t0] = a[t0] * h + b[t0]
        h = (a[t1] * a[t0]) * h + (a[t1] * b[t0] + b[t1])
        hs[t1] = h
    emit(k, jnp.stack(hs))
    return h


def _lru_fwd_kernel(ux_ref, prev_ref, next_ref, cw_ref, cb_ref, wg_ref, bg_ref, lam_ref,
                    hf_ref, xc_ref, xc_scr, a_scr, b_scr, h_scr, *, tt):
    i = pl.program_id(1)
    xc = _lru_conv(ux_ref, prev_ref, next_ref, cw_ref, cb_ref, tt=tt,
                   first=i == 0, last=i == pl.num_programs(1) - 1)
    xc_ref[0] = xc
    xc_scr[...] = xc.reshape(tt * LRU_BLOCKS, LRU_BD)
    _lru_gates(xc_scr, wg_ref, bg_ref, lam_ref, a_scr, b_scr, tt=tt)

    @pl.when(i == 0)
    def _():
        h_scr[...] = jnp.zeros(h_scr.shape, _F32)

    def emit(k, hs):
        hf_ref[0, pl.ds(k * SCAN_BLOCK, SCAN_BLOCK)] = hs

    h_scr[...] = lax.fori_loop(0, tt // SCAN_BLOCK, lambda k, h: _scan_block(
        k, h, a_scr=a_scr, b_scr=b_scr, reverse=False, emit=emit), h_scr[...])


def _lru_bwd_kernel(xc_ref, wg_ref, bg_ref, lam_ref, hf_ref, ug_ref,
                    out_ref, a_scr, b_scr, h_scr, g_scr, *, tt):
    _lru_gates(xc_ref.at[0], wg_ref, bg_ref, lam_ref, a_scr, b_scr, tt=tt)

    @pl.when(pl.program_id(1) == 0)
    def _():
        h_scr[...] = jnp.zeros(h_scr.shape, _F32)

    g_scr[...] = jax.nn.gelu(ug_ref[0])

    def emit(k, hs):
        toks = pl.ds(k * SCAN_BLOCK, SCAN_BLOCK)
        out_ref[0, toks] = (hf_ref[0, toks] + hs) * g_scr[toks]

    nb = tt // SCAN_BLOCK
    h_scr[...] = lax.fori_loop(0, nb, lambda i, h: _scan_block(
        nb - 1 - i, h, a_scr=a_scr, b_scr=b_scr, reverse=True, emit=emit), h_scr[...])


def _lru(ux, ug, conv_w, conv_b, wg, bg, lam, tt):
    B, S = ux.shape[:2]
    nt = S // tt
    nh = tt // SUBLANES
    tile4 = (1, tt, LRU_BLOCKS, LRU_BD)
    halo4 = (1, SUBLANES, LRU_BLOCKS, LRU_BD)
    rows3 = (1, tt * LRU_BLOCKS, LRU_BD)
    last_halo = S // SUBLANES - 1
    flat = lambda: pltpu.VMEM((tt * LRU_BLOCKS, LRU_BD), _F32)
    carry = pltpu.VMEM((LRU_BLOCKS, LRU_BD), _F32)
    conv_consts = [_const_spec((CONV_W, LRU_BLOCKS, LRU_BD)), _const_spec((LRU_BLOCKS, LRU_BD))]
    gate_consts = [_const_spec((LRU_BLOCKS, LRU_BD, 2 * LRU_BD)),
                   _const_spec((LRU_BLOCKS, 1, 2 * LRU_BD)), _const_spec((LRU_BLOCKS, 1, LRU_BD))]
    fwd_tile = pl.BlockSpec(tile4, lambda b, i: (b, i, 0, 0))
    bwd_tile = pl.BlockSpec(tile4, lambda b, i: (b, nt - 1 - i, 0, 0))
    shape4 = jax.ShapeDtypeStruct((B, S, LRU_BLOCKS, LRU_BD), _F32)
    hf, xc = pl.pallas_call(
        functools.partial(_lru_fwd_kernel, tt=tt),
        grid=(B, nt),
        in_specs=[fwd_tile,
                  pl.BlockSpec(halo4, lambda b, i: (b, jnp.maximum(i * nh - 1, 0), 0, 0)),
                  pl.BlockSpec(halo4, lambda b, i: (b, jnp.minimum((i + 1) * nh, last_halo), 0, 0))]
                 + conv_consts + gate_consts,
        out_specs=[fwd_tile, fwd_tile],
        out_shape=[shape4, shape4],
        scratch_shapes=[flat(), flat(), flat(), carry],
        compiler_params=_params(("parallel", "arbitrary")),
        name="lru_forward",
    )(ux, ux, ux, conv_w, conv_b, wg[0], bg[0], lam[0])
    return pl.pallas_call(
        functools.partial(_lru_bwd_kernel, tt=tt),
        grid=(B, nt),
        in_specs=[pl.BlockSpec(rows3, lambda b, i: (b, nt - 1 - i, 0))] + gate_consts + [bwd_tile, bwd_tile],
        out_specs=bwd_tile,
        out_shape=shape4,
        scratch_shapes=[flat(), flat(), carry, pltpu.VMEM((tt, LRU_BLOCKS, LRU_BD), _F32)],
        compiler_params=_params(("parallel", "arbitrary")),
        name="lru_backward",
    )(xc.reshape(B, S * LRU_BLOCKS, LRU_BD), wg[1], bg[1], lam[1], hf, ug)


def _out_proj_kernel(x_ref, attn_ref, lru_ref, ga_ref, gl_ref, wa_ref, wl_ref, y_ref):
    a = _rms(attn_ref[0].astype(_F32), ga_ref[...]).astype(_BF16)
    tm = x_ref.shape[1]
    lru = jnp.concatenate([lru_ref[0, pl.ds(n, tm, stride=LRU_BLOCKS), :] for n in range(LRU_BLOCKS)],
                          axis=1)
    l = _rms(lru, gl_ref[...]).astype(_BF16)
    y = jnp.dot(a, wa_ref[...], preferred_element_type=_F32)
    y = y + jnp.dot(l, wl_ref[...], preferred_element_type=_F32)
    y_ref[0] = x_ref[0] + y


def _out_proj(x, attn, lru, ga, gl, w_out_a, w_out_l, tm):
    B, S, _ = x.shape
    tok = lambda w: pl.BlockSpec((1, tm, w), lambda b, i: (b, i, 0))
    return pl.pallas_call(
        _out_proj_kernel,
        grid=(B, S // tm),
        in_specs=[tok(D_MODEL), tok(ATTN_WIDTH), _lru_rows_spec(tm),
                  _const_spec((1, ATTN_WIDTH)), _const_spec((1, LRU_WIDTH)),
                  _const_spec((ATTN_WIDTH, D_MODEL)), _const_spec((LRU_WIDTH, D_MODEL))],
        out_specs=tok(D_MODEL),
        out_shape=jax.ShapeDtypeStruct(x.shape, _F32),
        compiler_params=_params(("parallel", "parallel")),
        name="out_proj",
    )(x, attn, lru, ga, gl, w_out_a, w_out_l)


def _ffn_kernel(x_ref, g_ref, wg_ref, wu_ref, wd_ref, y_ref, h_scr):
    @pl.when(pl.program_id(2) == 0)
    def _():
        x = x_ref[0]
        h_scr[...] = _rms(x, g_ref[...]).astype(_BF16)
        y_ref[0] = x

    h = h_scr[...]
    gate = jnp.dot(h, wg_ref[...], preferred_element_type=_F32)
    up = jnp.dot(h, wu_ref[...], preferred_element_type=_F32)
    act = (jax.nn.silu(gate) * up).astype(_BF16)
    y_ref[0] += jnp.dot(act, wd_ref[...], preferred_element_type=_F32)


def _ffn(x, g, w_gate, w_up, w_down, tm, tf):
    B, S, _ = x.shape
    tok = pl.BlockSpec((1, tm, D_MODEL), lambda b, i, j: (b, i, 0))
    return pl.pallas_call(
        _ffn_kernel,
        grid=(B, S // tm, D_FF // tf),
        in_specs=[tok, pl.BlockSpec((1, D_MODEL), lambda b, i, j: (0, 0)),
                  pl.BlockSpec((D_MODEL, tf), lambda b, i, j: (0, j)),
                  pl.BlockSpec((D_MODEL, tf), lambda b, i, j: (0, j)),
                  pl.BlockSpec((tf, D_MODEL), lambda b, i, j: (j, 0))],
        out_specs=tok,
        out_shape=jax.ShapeDtypeStruct(x.shape, _F32),
        scratch_shapes=[pltpu.VMEM((tm, D_MODEL), _BF16)],
        compiler_params=_params(("parallel", "parallel", "arbitrary")),
        name="ffn",
    )(x, g, w_gate, w_up, w_down)


def _swap_halves(w):
    half = ROPE // 2
    return jnp.concatenate([-w[..., half:], w[..., :half]], axis=-1)


def _swap_gain(g):
    half = ROPE // 2
    return jnp.concatenate([g[half:], g[:half]])


def _gain_rows(g):
    pad = jnp.zeros((LANES - ROPE,), _F32)
    pe = g[NOPE:]
    return jnp.stack([g[:NOPE], jnp.concatenate([pe, pad]),
                      jnp.concatenate([_swap_gain(pe), pad])])


def _prepare(attn_norm_g, w_in, q_a_norm_g, w_uq, kv_a_norm_g, w_ukv, q_norm_g, k_norm_g,
             conv_w, conv_b, w_rg_r, b_rg_r, w_rg_i, b_rg_i, lru_lambda,
             attn_out_norm_g, lru_out_norm_g, w_out, ffn_norm_g, w_gate, w_up, w_down):
    k_pe = w_in[:, OFF_KV:OFF_KV + ROPE]
    w_head = jnp.concatenate([w_in[:, :OFF_KV + ROPE], _swap_halves(k_pe)], axis=1).astype(_BF16)
    w_lru = w_in[:, OFF_KV + ROPE:].astype(_BF16)
    wq = w_uq.reshape(Q_RANK, N_HEADS, QK_DIM)
    w_uq_ext = jnp.concatenate([wq, _swap_halves(wq[..., NOPE:])], axis=-1)
    w_uq_ext = w_uq_ext.reshape(Q_RANK, N_HEADS * HEAD_PAD).astype(_BF16)
    wkv = w_ukv.reshape(KV_RANK, N_HEADS, NOPE + V_DIM)
    w_uk = wkv[..., :NOPE].reshape(KV_RANK, N_HEADS * NOPE).astype(_BF16)
    w_uvt = wkv[..., NOPE:].reshape(KV_RANK, ATTN_WIDTH).T.astype(_BF16)
    wg = (0.5 * jnp.concatenate([w_rg_r, w_rg_i], axis=-1)).astype(_BF16)
    bg = 0.5 * jnp.concatenate([b_rg_r, b_rg_i], axis=-1)[:, :, None, :]
    lam = lru_lambda.reshape(2, LRU_BLOCKS, 1, LRU_BD)
    return dict(
        g_in=attn_norm_g[None], w_head=w_head, w_lru=w_lru, g_qa=q_a_norm_g[None], w_uq_ext=w_uq_ext,
        g_kva=kv_a_norm_g[None], w_uk=w_uk, w_uvt=w_uvt,
        gq=_gain_rows(q_norm_g), gk=_gain_rows(k_norm_g),
        conv_w=conv_w.reshape(CONV_W, LRU_BLOCKS, LRU_BD), conv_b=conv_b.reshape(LRU_BLOCKS, LRU_BD),
        wg=wg, bg=bg, lam=lam,
        ga=attn_out_norm_g[None], gl=lru_out_norm_g[None],
        w_out_a=w_out[:ATTN_WIDTH].astype(_BF16), w_out_l=w_out[ATTN_WIDTH:].astype(_BF16),
        g_ffn=ffn_norm_g[None], w_gate=w_gate.astype(_BF16), w_up=w_up.astype(_BF16),
        w_down=w_down.astype(_BF16))


def _rope_tables(S):
    half = ROPE // 2
    inv_freq = ROPE_BASE ** (-2.0 * np.arange(half, dtype=np.float64) / ROPE)
    ang = np.arange(S, dtype=np.float64)[:, None] * inv_freq[None, :]
    pad = np.zeros((S, LANES - ROPE))
    cos = np.concatenate([np.cos(ang), np.cos(ang), pad], axis=1)
    sin = np.concatenate([np.sin(ang), np.sin(ang), pad], axis=1)
    return jnp.asarray(cos, _F32), jnp.asarray(sin, _F32)


def _tiles(S):
    return dict(in_proj=min(512, S), attn_q=min(512, S), attn_k=min(512, S // 2),
                attn_span=min(ATTN_SPAN, S),
                lru=min(1024, S), out_proj=min(512, S), ffn=min(512, S), ffn_cols=512)


def _encoder_layer(x, p):
    B, S, _ = x.shape
    t = _tiles(S)
    cos, sin = _rope_tables(S)
    q, k, vt, ux, ug = _in_proj(x, cos, sin, p["g_in"], p["w_head"], p["w_lru"], p["g_qa"], p["w_uq_ext"],
                               p["g_kva"], p["w_uk"], p["w_uvt"], p["gq"], p["gk"], t["in_proj"])
    attn = _attention(q, k, vt, t["attn_q"], t["attn_k"], t["attn_span"])
    shape4 = (B, S, LRU_BLOCKS, LRU_BD)
    lru = _lru(ux.reshape(shape4), ug.reshape(shape4), p["conv_w"], p["conv_b"],
               p["wg"], p["bg"], p["lam"], t["lru"]).reshape(B, S * LRU_BLOCKS, LRU_BD)
    x1 = _out_proj(x, attn, lru, p["ga"], p["gl"], p["w_out_a"], p["w_out_l"], t["out_proj"])
    return _ffn(x1, p["g_ffn"], p["w_gate"], p["w_up"], p["w_down"], t["ffn"], t["ffn_cols"])


def kernel(x_prompt, x_sample, attn_norm_g, w_in, q_a_norm_g, w_uq, kv_a_norm_g, w_ukv, q_norm_g, k_norm_g, conv_w, conv_b, w_rg_r, b_rg_r, w_rg_i, b_rg_i, lru_lambda, attn_out_norm_g, lru_out_norm_g, w_out, ffn_norm_g, w_gate, w_up, w_down):
    depth = attn_norm_g.shape[0]
    weights = (attn_norm_g, w_in, q_a_norm_g, w_uq, kv_a_norm_g, w_ukv, q_norm_g, k_norm_g,
               conv_w, conv_b, w_rg_r, b_rg_r, w_rg_i, b_rg_i, lru_lambda,
               attn_out_norm_g, lru_out_norm_g, w_out, ffn_norm_g, w_gate, w_up, w_down)
    y_prompt, y_sample = x_prompt, x_sample
    for layer in range(depth):
        p = _prepare(*(w[layer] for w in weights))
        y_prompt = _encoder_layer(y_prompt, p)
        y_sample = _encoder_layer(y_sample, p)
    return (y_prompt, y_sample)
```

```python
import functools

import jax
import jax.numpy as jnp
import numpy as np
from jax import lax
from jax.experimental import pallas as pl
from jax.experimental.pallas import tpu as pltpu

D_MODEL = 2048
N_HEADS = 8
NOPE = 128
ROPE = 64
QK_DIM = NOPE + ROPE
V_DIM = 128
BF16_SUBLANES = 16
V_EXT = V_DIM + BF16_SUBLANES
ATTN_WIDTH = N_HEADS * V_DIM
Q_RANK = 512
KV_RANK = 256
ROPE_BASE = 10000.0
LRU_WIDTH = D_MODEL - ATTN_WIDTH
LRU_BLOCKS = 8
LRU_BD = LRU_WIDTH // LRU_BLOCKS
CONV_W = 4
LRU_C = 8.0
D_FF = 5632
EPS = 1e-6
LOG2_E = 1.4426950408889634

LANES = 128
SUBLANES = 8
HEAD_PAD = 2 * LANES
OFF_Q = Q_RANK
OFF_KV = OFF_Q + KV_RANK
OFF_KG = OFF_KV + LANES
VMEM_LIMIT = 56 * 1024 * 1024

_F32 = jnp.float32
_BF16 = jnp.bfloat16


def _const_spec(shape):
    return pl.BlockSpec(shape, lambda *_: (0,) * len(shape), pipeline_mode=pl.Buffered(1))


def _params(sem):
    return pltpu.CompilerParams(dimension_semantics=sem, vmem_limit_bytes=VMEM_LIMIT)


def _rms(x, g):
    return x * lax.rsqrt(jnp.mean(x * x, axis=-1, keepdims=True) + EPS) * g


def _lru_rows_spec(tm):
    return pl.BlockSpec((1, tm * LRU_BLOCKS, LRU_BD), lambda b, i: (b, i, 0))


def _in_proj_kernel(x_ref, cos_ref, sin_ref, g_in_ref, w_head_ref, w_lru_ref, g_qa_ref, w_uq_ref,
                    g_kva_ref, w_uk_ref, w_uvt_ref, gq_ref, gk_ref,
                    q_ref, k_ref, vt_ref, ux_ref, ug_ref):
    x = x_ref[0]
    h = _rms(x, g_in_ref[...]).astype(_BF16)
    u = jnp.dot(h, w_head_ref[...], preferred_element_type=_F32)

    cos = cos_ref[...]
    sin = sin_ref[...]
    lane = lax.broadcasted_iota(jnp.int32, (1, LANES), 1)
    rope_mask = (lane < ROPE).astype(_F32)
    scale = QK_DIM ** -0.5 * LOG2_E

    def rotary(grp, g_rows):
        return grp * (g_rows[1:2] * cos) + pltpu.roll(grp, ROPE, axis=1) * (g_rows[2:3] * sin)

    def head_rms(nope, grp, extra=None):
        ss = jnp.sum(nope * nope, axis=-1, keepdims=True)
        ss = ss + (jnp.sum(grp * grp * rope_mask, axis=-1, keepdims=True) if extra is None else extra)
        return lax.rsqrt(ss * (1.0 / QK_DIM) + EPS)

    cq = _rms(u[:, :OFF_Q], g_qa_ref[...]).astype(_BF16)
    qf = jnp.dot(cq, w_uq_ref[...], preferred_element_type=_F32)
    gq = gq_ref[...]
    for hd in range(N_HEADS):
        nope = qf[:, hd * HEAD_PAD:hd * HEAD_PAD + NOPE]
        grp = qf[:, hd * HEAD_PAD + NOPE:(hd + 1) * HEAD_PAD]
        r = head_rms(nope, grp) * scale
        q_ref[0, hd, :, :NOPE] = (nope * r * gq[0:1]).astype(_BF16)
        q_ref[0, hd, :, NOPE:] = (rotary(grp, gq) * r).astype(_BF16)

    ckv = _rms(u[:, OFF_Q:OFF_KV], g_kva_ref[...]).astype(_BF16)
    vt = lax.dot_general(w_uvt_ref[...], ckv, (((1,), (1,)), ((), ())), preferred_element_type=_F32)
    ones_rows = (lax.broadcasted_iota(jnp.int32, (V_EXT - V_DIM, vt.shape[1]), 0) == 0).astype(_BF16)
    for hd in range(N_HEADS):
        vt_ref[0, hd, :V_DIM, :] = vt[hd * V_DIM:(hd + 1) * V_DIM].astype(_BF16)
        vt_ref[0, hd, V_DIM:, :] = ones_rows
    kn = jnp.dot(ckv, w_uk_ref[...], preferred_element_type=_F32)
    gk = gk_ref[...]
    kgrp = u[:, OFF_KV:OFF_KG]
    ss_pe = jnp.sum(kgrp * kgrp * rope_mask, axis=-1, keepdims=True)
    krot = rotary(kgrp, gk)
    for hd in range(N_HEADS):
        nope = kn[:, hd * NOPE:(hd + 1) * NOPE]
        r = head_rms(nope, None, ss_pe)
        k_ref[0, hd, :, :NOPE] = (nope * r * gk[0:1]).astype(_BF16)
        k_ref[0, hd, :, NOPE:] = (krot * r).astype(_BF16)

    ul = jnp.dot(h, w_lru_ref[...], preferred_element_type=_F32)
    for n in range(LRU_BLOCKS):
        rows = pl.ds(n, x.shape[0], stride=LRU_BLOCKS)
        ux_ref[0, rows, :] = ul[:, n * LRU_BD:(n + 1) * LRU_BD]
        ug_ref[0, rows, :] = ul[:, LRU_WIDTH + n * LRU_BD:LRU_WIDTH + (n + 1) * LRU_BD]


def _in_proj(x, cos, sin, g_in, w_head, w_lru, g_qa, w_uq_ext, g_kva, w_uk, w_uvt, gq, gk, tm):
    B, S, _ = x.shape
    grid = (B, S // tm)
    tok = lambda w: pl.BlockSpec((1, tm, w), lambda b, i: (b, i, 0))
    head = pl.BlockSpec((1, N_HEADS, tm, HEAD_PAD), lambda b, i: (b, 0, i, 0))
    tab = pl.BlockSpec((tm, LANES), lambda b, i: (i, 0))
    return pl.pallas_call(
        _in_proj_kernel,
        grid=grid,
        in_specs=[tok(D_MODEL), tab, tab,
                  _const_spec((1, D_MODEL)), _const_spec((D_MODEL, OFF_KG)), _const_spec((D_MODEL, 2 * LRU_WIDTH)),
                  _const_spec((1, Q_RANK)), _const_spec((Q_RANK, N_HEADS * HEAD_PAD)),
                  _const_spec((1, KV_RANK)), _const_spec((KV_RANK, N_HEADS * NOPE)),
                  _const_spec((ATTN_WIDTH, KV_RANK)),
                  _const_spec((3, LANES)), _const_spec((3, LANES))],
        out_specs=[head, head, pl.BlockSpec((1, N_HEADS, V_EXT, tm), lambda b, i: (b, 0, 0, i)),
                   _lru_rows_spec(tm), _lru_rows_spec(tm)],
        out_shape=[jax.ShapeDtypeStruct((B, N_HEADS, S, HEAD_PAD), _BF16),
                   jax.ShapeDtypeStruct((B, N_HEADS, S, HEAD_PAD), _BF16),
                   jax.ShapeDtypeStruct((B, N_HEADS, V_EXT, S), _BF16),
                   jax.ShapeDtypeStruct((B, S * LRU_BLOCKS, LRU_BD), _F32),
                   jax.ShapeDtypeStruct((B, S * LRU_BLOCKS, LRU_BD), _F32)],
        compiler_params=_params(("parallel", "parallel")),
        name="in_proj",
    )(x, cos, sin, g_in, w_head, w_lru, g_qa, w_uq_ext, g_kva, w_uk, w_uvt, gq, gk)


def _attention_kernel(q_ref, k_ref, vt_ref, o_ref, s_scr, qt_scr, *, tq, tk, unroll):
    n = k_ref.shape[2] // tk
    nq = q_ref.shape[2] // tq

    def load_qt(qi, slot):
        qt_scr[slot] = q_ref[0, 0, pl.ds(pl.multiple_of(qi * tq, tq), tq), :].T

    def scores(qslot, j, slot):
        rows = pl.ds(pl.multiple_of(j * tk, tk), tk)
        s = jnp.dot(k_ref[0, 0, rows, :], qt_scr[qslot], preferred_element_type=_F32)
        s_scr[slot] = s
        return jnp.max(s, axis=0, keepdims=True)

    def update(j, slot, m, acc, cmax):
        cols = pl.ds(pl.multiple_of(j * tk, tk), tk)
        m_new = jnp.maximum(m, cmax)
        p = jnp.exp2(s_scr[slot] - m_new).astype(_BF16)
        acc = jnp.exp2(m - m_new) * acc + jnp.dot(vt_ref[0, 0, :, cols], p,
                                                  preferred_element_type=_F32)
        return m_new, acc

    def tile(qi, c0):
        qslot = qi % 2
        load_qt(jnp.minimum(qi + 1, nq - 1), 1 - qslot)

        def body(jj, carry):
            m, acc, c0 = carry
            j = 2 * jj
            c1 = scores(qslot, j + 1, 1)
            m, acc = update(j, 0, m, acc, c0)
            wrap = j + 2 == n
            c0 = scores(jnp.where(wrap, 1 - qslot, qslot), jnp.where(wrap, 0, j + 2), 0)
            m, acc = update(j + 1, 1, m, acc, c1)
            return m, acc, c0

        carry = (jnp.full((1, tq), -jnp.inf, _F32), jnp.zeros((V_EXT, tq), _F32), c0)
        m, acc, c_next = lax.fori_loop(0, n // 2, body, carry, unroll=unroll)
        out = acc[:V_DIM] / acc[V_DIM:V_DIM + 1]
        o_ref[0, pl.ds(pl.multiple_of(qi * tq, tq), tq), :] = out.T.astype(o_ref.dtype)
        return c_next

    load_qt(0, 0)
    lax.fori_loop(0, nq, tile, scores(0, 0, 0))


MAX_UNROLL = 8
ATTN_SPAN = 8192


def _attention(q, k, vt, tq, tk, span):
    B, H, S, _ = q.shape
    pairs = S // tk // 2
    assert S % (2 * tk) == 0 and S % span == 0 and span % tq == 0
    unroll = max(u for u in range(1, MAX_UNROLL + 1) if pairs % u == 0)
    return pl.pallas_call(
        functools.partial(_attention_kernel, tq=tq, tk=tk, unroll=unroll),
        grid=(B, H, S // span),
        in_specs=[pl.BlockSpec((1, 1, span, HEAD_PAD), lambda b, h, i: (b, h, i, 0)),
                  pl.BlockSpec((1, 1, S, HEAD_PAD), lambda b, h, i: (b, h, 0, 0)),
                  pl.BlockSpec((1, 1, V_EXT, S), lambda b, h, i: (b, h, 0, 0))],
        out_specs=pl.BlockSpec((1, span, V_DIM), lambda b, h, i: (b, i, h)),
        out_shape=jax.ShapeDtypeStruct((B, S, ATTN_WIDTH), _BF16),
        scratch_shapes=[pltpu.VMEM((2, tk, tq), _F32), pltpu.VMEM((2, HEAD_PAD, tq), _BF16)],
        compiler_params=_params(("parallel", "parallel", "arbitrary")),
        name="attention",
    )(q, k, vt)


def _lru_conv(ux_ref, prev_ref, next_ref, cw_ref, cb_ref, *, tt, first, last):
    prev = jnp.where(first, 0.0, prev_ref[0])
    nxt = jnp.where(last, 0.0, next_ref[0])
    xe = jnp.concatenate([prev[SUBLANES - 1:], ux_ref[0], nxt[:2]], axis=0)
    cw = cw_ref[...]
    xc = cb_ref[...][None]
    for tap in range(CONV_W):
        xc = xc + xe[tap:tap + tt] * cw[tap][None]
    return xc


def _lru_gates(xc_scr, wg_ref, bg_ref, lam_ref, a_scr, b_scr, *, tt):
    k1 = LRU_C * jax.nn.softplus(-lam_ref[...])
    k2 = -LOG2_E * k1
    for n in range(LRU_BLOCKS):
        rows = pl.ds(n, tt, stride=LRU_BLOCKS)
        xn = xc_scr[rows, :]
        z = jnp.dot(xn.astype(_BF16), wg_ref[n], preferred_element_type=_F32) + bg_ref[n]
        sig = 0.5 * jnp.tanh(z) + 0.5
        r = sig[:, :LRU_BD]
        i = sig[:, LRU_BD:]
        a = jnp.exp2(r * k2[n])
        a_scr[rows, :] = a
        y = jnp.tanh(r * k1[n]) * (1.0 + a * a)
        b_scr[rows, :] = jnp.where(y > 0.0, y * lax.rsqrt(y), 0.0) * (i * xn)


SCAN_BLOCK = 8


def _scan_block(k, h, *, a_scr, b_scr, reverse, emit):
    rows = pl.ds(pl.multiple_of(k * SCAN_BLOCK * LRU_BLOCKS, SCAN_BLOCK * LRU_BLOCKS), SCAN_BLOCK * LRU_BLOCKS)
    a = a_scr[rows, :].reshape(SCAN_BLOCK, LRU_BLOCKS, LRU_BD)
    b = b_scr[rows, :].reshape(SCAN_BLOCK, LRU_BLOCKS, LRU_BD)
    order = range(SCAN_BLOCK - 1, -1, -1) if reverse else range(SCAN_BLOCK)
    hs = [None] * SCAN_BLOCK
    for t0, t1 in zip(order[0::2], order[1::2]):
        hs[t0] = a[t0] * h + b[t0]
        h = (a[t1] * a[t0]) * h + (a[t1] * b[t0] + b[t1])
        hs[t1] = h
    emit(k, jnp.stack(hs))
    return h


def _lru_fwd_kernel(ux_ref, prev_ref, next_ref, cw_ref, cb_ref, wg_ref, bg_ref, lam_ref,
                    hf_ref, xc_ref, xc_scr, a_scr, b_scr, h_scr, *, tt):
    i = pl.program_id(1)
    xc = _lru_conv(ux_ref, prev_ref, next_ref, cw_ref, cb_ref, tt=tt,
                   first=i == 0, last=i == pl.num_programs(1) - 1)
    xc_ref[0] = xc
    xc_scr[...] = xc.reshape(tt * LRU_BLOCKS, LRU_BD)
    _lru_gates(xc_scr, wg_ref, bg_ref, lam_ref, a_scr, b_scr, tt=tt)

    @pl.when(i == 0)
    def _():
        h_scr[...] = jnp.zeros(h_scr.shape, _F32)

    def emit(k, hs):
        hf_ref[0, pl.ds(k * SCAN_BLOCK, SCAN_BLOCK)] = hs

    h_scr[...] = lax.fori_loop(0, tt // SCAN_BLOCK, lambda k, h: _scan_block(
        k, h, a_scr=a_scr, b_scr=b_scr, reverse=False, emit=emit), h_scr[...])


def _lru_bwd_kernel(xc_ref, wg_ref, bg_ref, lam_ref, hf_ref, ug_ref,
                    out_ref, a_scr, b_scr, h_scr, g_scr, *, tt):
    _lru_gates(xc_ref.at[0], wg_ref, bg_ref, lam_ref, a_scr, b_scr, tt=tt)

    @pl.when(pl.program_id(1) == 0)
    def _():
        h_scr[...] = jnp.zeros(h_scr.shape, _F32)

    g_scr[...] = jax.nn.gelu(ug_ref[0])

    def emit(k, hs):
        toks = pl.ds(k * SCAN_BLOCK, SCAN_BLOCK)
        out_ref[0, toks] = (hf_ref[0, toks] + hs) * g_scr[toks]

    nb = tt // SCAN_BLOCK
    h_scr[...] = lax.fori_loop(0, nb, lambda i, h: _scan_block(
        nb - 1 - i, h, a_scr=a_scr, b_scr=b_scr, reverse=True, emit=emit), h_scr[...])


def _lru(ux, ug, conv_w, conv_b, wg, bg, lam, tt):
    B, S = ux.shape[:2]
    nt = S // tt
    nh = tt // SUBLANES
    tile4 = (1, tt, LRU_BLOCKS, LRU_BD)
    halo4 = (1, SUBLANES, LRU_BLOCKS, LRU_BD)
    rows3 = (1, tt * LRU_BLOCKS, LRU_BD)
    last_halo = S // SUBLANES - 1
    flat = lambda: pltpu.VMEM((tt * LRU_BLOCKS, LRU_BD), _F32)
    carry = pltpu.VMEM((LRU_BLOCKS, LRU_BD), _F32)
    conv_consts = [_const_spec((CONV_W, LRU_BLOCKS, LRU_BD)), _const_spec((LRU_BLOCKS, LRU_BD))]
    gate_consts = [_const_spec((LRU_BLOCKS, LRU_BD, 2 * LRU_BD)),
                   _const_spec((LRU_BLOCKS, 1, 2 * LRU_BD)), _const_spec((LRU_BLOCKS, 1, LRU_BD))]
    fwd_tile = pl.BlockSpec(tile4, lambda b, i: (b, i, 0, 0))
    bwd_tile = pl.BlockSpec(tile4, lambda b, i: (b, nt - 1 - i, 0, 0))
    shape4 = jax.ShapeDtypeStruct((B, S, LRU_BLOCKS, LRU_BD), _F32)
    hf, xc = pl.pallas_call(
        functools.partial(_lru_fwd_kernel, tt=tt),
        grid=(B, nt),
        in_specs=[fwd_tile,
                  pl.BlockSpec(halo4, lambda b, i: (b, jnp.maximum(i * nh - 1, 0), 0, 0)),
                  pl.BlockSpec(halo4, lambda b, i: (b, jnp.minimum((i + 1) * nh, last_halo), 0, 0))]
                 + conv_consts + gate_consts,
        out_specs=[fwd_tile, fwd_tile],
        out_shape=[shape4, shape4],
        scratch_shapes=[flat(), flat(), flat(), carry],
        compiler_params=_params(("parallel", "arbitrary")),
        name="lru_forward",
    )(ux, ux, ux, conv_w, conv_b, wg[0], bg[0], lam[0])
    return pl.pallas_call(
        functools.partial(_lru_bwd_kernel, tt=tt),
        grid=(B, nt),
        in_specs=[pl.BlockSpec(rows3, lambda b, i: (b, nt - 1 - i, 0))] + gate_consts + [bwd_tile, bwd_tile],
        out_specs=bwd_tile,
        out_shape=shape4,
        scratch_shapes=[flat(), flat(), carry, pltpu.VMEM((tt, LRU_BLOCKS, LRU_BD), _F32)],
        compiler_params=_params(("parallel", "arbitrary")),
        name="lru_backward",
    )(xc.reshape(B, S * LRU_BLOCKS, LRU_BD), wg[1], bg[1], lam[1], hf, ug)


def _out_proj_kernel(x_ref, attn_ref, lru_ref, ga_ref, gl_ref, wa_ref, wl_ref, y_ref):
    a = _rms(attn_ref[0].astype(_F32), ga_ref[...]).astype(_BF16)
    tm = x_ref.shape[1]
    lru = jnp.concatenate([lru_ref[0, pl.ds(n, tm, stride=LRU_BLOCKS), :] for n in range(LRU_BLOCKS)],
                          axis=1)
    l = _rms(lru, gl_ref[...]).astype(_BF16)
    y = jnp.dot(a, wa_ref[...], preferred_element_type=_F32)
    y = y + jnp.dot(l, wl_ref[...], preferred_element_type=_F32)
    y_ref[0] = x_ref[0] + y


def _out_proj(x, attn, lru, ga, gl, w_out_a, w_out_l, tm):
    B, S, _ = x.shape
    tok = lambda w: pl.BlockSpec((1, tm, w), lambda b, i: (b, i, 0))
    return pl.pallas_call(
        _out_proj_kernel,
        grid=(B, S // tm),
        in_specs=[tok(D_MODEL), tok(ATTN_WIDTH), _lru_rows_spec(tm),
                  _const_spec((1, ATTN_WIDTH)), _const_spec((1, LRU_WIDTH)),
                  _const_spec((ATTN_WIDTH, D_MODEL)), _const_spec((LRU_WIDTH, D_MODEL))],
        out_specs=tok(D_MODEL),
        out_shape=jax.ShapeDtypeStruct(x.shape, _F32),
        compiler_params=_params(("parallel", "parallel")),
        name="out_proj",
    )(x, attn, lru, ga, gl, w_out_a, w_out_l)


def _ffn_kernel(x_ref, g_ref, wg_ref, wu_ref, wd_ref, y_ref, h_scr):
    @pl.when(pl.program_id(2) == 0)
    def _():
        x = x_ref[0]
        h_scr[...] = _rms(x, g_ref[...]).astype(_BF16)
        y_ref[0] = x

    h = h_scr[...]
    gate = jnp.dot(h, wg_ref[...], preferred_element_type=_F32)
    up = jnp.dot(h, wu_ref[...], preferred_element_type=_F32)
    act = (jax.nn.silu(gate) * up).astype(_BF16)
    y_ref[0] += jnp.dot(act, wd_ref[...], preferred_element_type=_F32)


def _ffn(x, g, w_gate, w_up, w_down, tm, tf):
    B, S, _ = x.shape
    tok = pl.BlockSpec((1, tm, D_MODEL), lambda b, i, j: (b, i, 0))
    return pl.pallas_call(
        _ffn_kernel,
        grid=(B, S // tm, D_FF // tf),
        in_specs=[tok, pl.BlockSpec((1, D_MODEL), lambda b, i, j: (0, 0)),
                  pl.BlockSpec((D_MODEL, tf), lambda b, i, j: (0, j)),
                  pl.BlockSpec((D_MODEL, tf), lambda b, i, j: (0, j)),
                  pl.BlockSpec((tf, D_MODEL), lambda b, i, j: (j, 0))],
        out_specs=tok,
        out_shape=jax.ShapeDtypeStruct(x.shape, _F32),
        scratch_shapes=[pltpu.VMEM((tm, D_MODEL), _BF16)],
        compiler_params=_params(("parallel", "parallel", "arbitrary")),
        name="ffn",
    )(x, g, w_gate, w_up, w_down)


def _swap_halves(w):
    half = ROPE // 2
    return jnp.concatenate([-w[..., half:], w[..., :half]], axis=-1)


def _swap_gain(g):
    half = ROPE // 2
    return jnp.concatenate([g[half:], g[:half]])


def _gain_rows(g):
    pad = jnp.zeros((LANES - ROPE,), _F32)
    pe = g[NOPE:]
    return jnp.stack([g[:NOPE], jnp.concatenate([pe, pad]),
                      jnp.concatenate([_swap_gain(pe), pad])])


def _prepare(attn_norm_g, w_in, q_a_norm_g, w_uq, kv_a_norm_g, w_ukv, q_norm_g, k_norm_g,
             conv_w, conv_b, w_rg_r, b_rg_r, w_rg_i, b_rg_i, lru_lambda,
             attn_out_norm_g, lru_out_norm_g, w_out, ffn_norm_g, w_gate, w_up, w_down):
    k_pe = w_in[:, OFF_KV:OFF_KV + ROPE]
    w_head = jnp.concatenate([w_in[:, :OFF_KV + ROPE], _swap_halves(k_pe)], axis=1).astype(_BF16)
    w_lru = w_in[:, OFF_KV + ROPE:].astype(_BF16)
    wq = w_uq.reshape(Q_RANK, N_HEADS, QK_DIM)
    w_uq_ext = jnp.concatenate([wq, _swap_halves(wq[..., NOPE:])], axis=-1)
    w_uq_ext = w_uq_ext.reshape(Q_RANK, N_HEADS * HEAD_PAD).astype(_BF16)
    wkv = w_ukv.reshape(KV_RANK, N_HEADS, NOPE + V_DIM)
    w_uk = wkv[..., :NOPE].reshape(KV_RANK, N_HEADS * NOPE).astype(_BF16)
    w_uvt = wkv[..., NOPE:].reshape(KV_RANK, ATTN_WIDTH).T.astype(_BF16)
    wg = (0.5 * jnp.concatenate([w_rg_r, w_rg_i], axis=-1)).astype(_BF16)
    bg = 0.5 * jnp.concatenate([b_rg_r, b_rg_i], axis=-1)[:, :, None, :]
    lam = lru_lambda.reshape(2, LRU_BLOCKS, 1, LRU_BD)
    return dict(
        g_in=attn_norm_g[None], w_head=w_head, w_lru=w_lru, g_qa=q_a_norm_g[None], w_uq_ext=w_uq_ext,
        g_kva=kv_a_norm_g[None], w_uk=w_uk, w_uvt=w_uvt,
        gq=_gain_rows(q_norm_g), gk=_gain_rows(k_norm_g),
        conv_w=conv_w.reshape(CONV_W, LRU_BLOCKS, LRU_BD), conv_b=conv_b.reshape(LRU_BLOCKS, LRU_BD),
        wg=wg, bg=bg, lam=lam,
        ga=attn_out_norm_g[None], gl=lru_out_norm_g[None],
        w_out_a=w_out[:ATTN_WIDTH].astype(_BF16), w_out_l=w_out[ATTN_WIDTH:].astype(_BF16),
        g_ffn=ffn_norm_g[None], w_gate=w_gate.astype(_BF16), w_up=w_up.astype(_BF16),
        w_down=w_down.astype(_BF16))


def _rope_tables(S):
    half = ROPE // 2
    inv_freq = ROPE_BASE ** (-2.0 * np.arange(half, dtype=np.float64) / ROPE)
    ang = np.arange(S, dtype=np.float64)[:, None] * inv_freq[None, :]
    pad = np.zeros((S, LANES - ROPE))
    cos = np.concatenate([np.cos(ang), np.cos(ang), pad], axis=1)
    sin = np.concatenate([np.sin(ang), np.sin(ang), pad], axis=1)
    return jnp.asarray(cos, _F32), jnp.asarray(sin, _F32)


def _tiles(S):
    return dict(in_proj=min(512, S), attn_q=min(512, S), attn_k=min(512, S // 2),
                attn_span=min(ATTN_SPAN, S),
                lru=min(1024, S), out_proj=min(512, S), ffn=min(512, S), ffn_cols=512)


def _encoder_layer(x, p):
    B, S, _ = x.shape
    t = _tiles(S)
    cos, sin = _rope_tables(S)
    q, k, vt, ux, ug = _in_proj(x, cos, sin, p["g_in"], p["w_head"], p["w_lru"], p["g_qa"], p["w_uq_ext"],
                               p["g_kva"], p["w_uk"], p["w_uvt"], p["gq"], p["gk"], t["in_proj"])
    attn = _attention(q, k, vt, t["attn_q"], t["attn_k"], t["attn_span"])
    shape4 = (B, S, LRU_BLOCKS, LRU_BD)
    lru = _lru(ux.reshape(shape4), ug.reshape(shape4), p["conv_w"], p["conv_b"],
               p["wg"], p["bg"], p["lam"], t["lru"]).reshape(B, S * LRU_BLOCKS, LRU_BD)
    x1 = _out_proj(x, attn, lru, p["ga"], p["gl"], p["w_out_a"], p["w_out_l"], t["out_proj"])
    return _ffn(x1, p["g_ffn"], p["w_gate"], p["w_up"], p["w_down"], t["ffn"], t["ffn_cols"])


def kernel(x_prompt, x_sample, attn_norm_g, w_in, q_a_norm_g, w_uq, kv_a_norm_g, w_ukv, q_norm_g, k_norm_g, conv_w, conv_b, w_rg_r, b_rg_r, w_rg_i, b_rg_i, lru_lambda, attn_out_norm_g, lru_out_norm_g, w_out, ffn_norm_g, w_gate, w_up, w_down):
    depth = attn_norm_g.shape[0]
    weights = (attn_norm_g, w_in, q_a_norm_g, w_uq, kv_a_norm_g, w_ukv, q_norm_g, k_norm_g,
               conv_w, conv_b, w_rg_r, b_rg_r, w_rg_i, b_rg_i, lru_lambda,
               attn_out_norm_g, lru_out_norm_g, w_out, ffn_norm_g, w_gate, w_up, w_down)
    y_prompt, y_sample = x_prompt, x_sample
    for layer in range(depth):
        p = _prepare(*(w[layer] for w in weights))
        y_prompt = _encoder_layer(y_prompt, p)
        y_sample = _encoder_layer(y_sample, p)
    return (y_prompt, y_sample)
```

```python
import functools

import jax
import jax.numpy as jnp
import numpy as np
from jax import lax
from jax.experimental import pallas as pl
from jax.experimental.pallas import tpu as pltpu

D_MODEL = 2048
N_HEADS = 8
NOPE = 128
ROPE = 64
QK_DIM = NOPE + ROPE
V_DIM = 128
BF16_SUBLANES = 16
V_EXT = V_DIM + BF16_SUBLANES
ATTN_WIDTH = N_HEADS * V_DIM
Q_RANK = 512
KV_RANK = 256
ROPE_BASE = 10000.0
LRU_WIDTH = D_MODEL - ATTN_WIDTH
LRU_BLOCKS = 8
LRU_BD = LRU_WIDTH // LRU_BLOCKS
CONV_W = 4
LRU_C = 8.0
D_FF = 5632
EPS = 1e-6
LOG2_E = 1.4426950408889634

LANES = 128
SUBLANES = 8
HEAD_PAD = 2 * LANES
OFF_Q = Q_RANK
OFF_KV = OFF_Q + KV_RANK
OFF_KG = OFF_KV + LANES
VMEM_LIMIT = 56 * 1024 * 1024

_F32 = jnp.float32
_BF16 = jnp.bfloat16


def _const_spec(shape):
    return pl.BlockSpec(shape, lambda *_: (0,) * len(shape), pipeline_mode=pl.Buffered(1))


def _params(sem):
    return pltpu.CompilerParams(dimension_semantics=sem, vmem_limit_bytes=VMEM_LIMIT)


def _rms(x, g):
    return x * lax.rsqrt(jnp.mean(x * x, axis=-1, keepdims=True) + EPS) * g


def _lru_rows_spec(tm):
    return pl.BlockSpec((1, tm * LRU_BLOCKS, LRU_BD), lambda b, i: (b, i, 0))


def _in_proj_kernel(x_ref, cos_ref, sin_ref, g_in_ref, w_head_ref, w_lru_ref, g_qa_ref, w_uq_ref,
                    g_kva_ref, w_uk_ref, w_uvt_ref, gq_ref, gk_ref,
                    q_ref, k_ref, vt_ref, ux_ref, ug_ref):
    x = x_ref[0]
    h = _rms(x, g_in_ref[...]).astype(_BF16)
    u = jnp.dot(h, w_head_ref[...], preferred_element_type=_F32)

    cos = cos_ref[...]
    sin = sin_ref[...]
    lane = lax.broadcasted_iota(jnp.int32, (1, LANES), 1)
    rope_mask = (lane < ROPE).astype(_F32)
    scale = QK_DIM ** -0.5 * LOG2_E

    def rotary(grp, g_rows):
        return grp * (g_rows[1:2] * cos) + pltpu.roll(grp, ROPE, axis=1) * (g_rows[2:3] * sin)

    def head_rms(nope, grp, extra=None):
        ss = jnp.sum(nope * nope, axis=-1, keepdims=True)
        ss = ss + (jnp.sum(grp * grp * rope_mask, axis=-1, keepdims=True) if extra is None else extra)
        return lax.rsqrt(ss * (1.0 / QK_DIM) + EPS)

    cq = _rms(u[:, :OFF_Q], g_qa_ref[...]).astype(_BF16)
    qf = jnp.dot(cq, w_uq_ref[...], preferred_element_type=_F32)
    gq = gq_ref[...]
    for hd in range(N_HEADS):
        nope = qf[:, hd * HEAD_PAD:hd * HEAD_PAD + NOPE]
        grp = qf[:, hd * HEAD_PAD + NOPE:(hd + 1) * HEAD_PAD]
        r = head_rms(nope, grp) * scale
        q_ref[0, hd, :, :NOPE] = (nope * r * gq[0:1]).astype(_BF16)
        q_ref[0, hd, :, NOPE:] = (rotary(grp, gq) * r).astype(_BF16)

    ckv = _rms(u[:, OFF_Q:OFF_KV], g_kva_ref[...]).astype(_BF16)
    vt = lax.dot_general(w_uvt_ref[...], ckv, (((1,), (1,)), ((), ())), preferred_element_type=_F32)
    ones_rows = (lax.broadcasted_iota(jnp.int32, (V_EXT - V_DIM, vt.shape[1]), 0) == 0).astype(_BF16)
    for hd in range(N_HEADS):
        vt_ref[0, hd, :V_DIM, :] = vt[hd * V_DIM:(hd + 1) * V_DIM].astype(_BF16)
        vt_ref[0, hd, V_DIM:, :] = ones_rows
    kn = jnp.dot(ckv, w_uk_ref[...], preferred_element_type=_F32)
    gk = gk_ref[...]
    kgrp = u[:, OFF_KV:OFF_KG]
    ss_pe = jnp.sum(kgrp * kgrp * rope_mask, axis=-1, keepdims=True)
    krot = rotary(kgrp, gk)
    for hd in range(N_HEADS):
        nope = kn[:, hd * NOPE:(hd + 1) * NOPE]
        r = head_rms(nope, None, ss_pe)
        k_ref[0, hd, :, :NOPE] = (nope * r * gk[0:1]).astype(_BF16)
        k_ref[0, hd, :, NOPE:] = (krot * r).astype(_BF16)

    ul = jnp.dot(h, w_lru_ref[...], preferred_element_type=_F32)
    for n in range(LRU_BLOCKS):
        rows = pl.ds(n, x.shape[0], stride=LRU_BLOCKS)
        ux_ref[0, rows, :] = ul[:, n * LRU_BD:(n + 1) * LRU_BD]
        ug_ref[0, rows, :] = ul[:, LRU_WIDTH + n * LRU_BD:LRU_WIDTH + (n + 1) * LRU_BD]


def _in_proj(x, cos, sin, g_in, w_head, w_lru, g_qa, w_uq_ext, g_kva, w_uk, w_uvt, gq, gk, tm):
    B, S, _ = x.shape
    grid = (B, S // tm)
    tok = lambda w: pl.BlockSpec((1, tm, w), lambda b, i: (b, i, 0))
    head = pl.BlockSpec((1, N_HEADS, tm, HEAD_PAD), lambda b, i: (b, 0, i, 0))
    tab = pl.BlockSpec((tm, LANES), lambda b, i: (i, 0))
    return pl.pallas_call(
        _in_proj_kernel,
        grid=grid,
        in_specs=[tok(D_MODEL), tab, tab,
                  _const_spec((1, D_MODEL)), _const_spec((D_MODEL, OFF_KG)), _const_spec((D_MODEL, 2 * LRU_WIDTH)),
                  _const_spec((1, Q_RANK)), _const_spec((Q_RANK, N_HEADS * HEAD_PAD)),
                  _const_spec((1, KV_RANK)), _const_spec((KV_RANK, N_HEADS * NOPE)),
                  _const_spec((ATTN_WIDTH, KV_RANK)),
                  _const_spec((3, LANES)), _const_spec((3, LANES))],
        out_specs=[head, head, pl.BlockSpec((1, N_HEADS, V_EXT, tm), lambda b, i: (b, 0, 0, i)),
                   _lru_rows_spec(tm), _lru_rows_spec(tm)],
        out_shape=[jax.ShapeDtypeStruct((B, N_HEADS, S, HEAD_PAD), _BF16),
                   jax.ShapeDtypeStruct((B, N_HEADS, S, HEAD_PAD), _BF16),
                   jax.ShapeDtypeStruct((B, N_HEADS, V_EXT, S), _BF16),
                   jax.ShapeDtypeStruct((B, S * LRU_BLOCKS, LRU_BD), _F32),
                   jax.ShapeDtypeStruct((B, S * LRU_BLOCKS, LRU_BD), _F32)],
        compiler_params=_params(("parallel", "parallel")),
        name="in_proj",
    )(x, cos, sin, g_in, w_head, w_lru, g_qa, w_uq_ext, g_kva, w_uk, w_uvt, gq, gk)


def _attention_kernel(q_ref, k_ref, vt_ref, o_ref, s_scr, qt_scr, *, tq, tk, unroll):
    n = k_ref.shape[2] // tk
    nq = q_ref.shape[2] // tq

    def load_qt(qi, slot):
        qt_scr[slot] = q_ref[0, 0, pl.ds(pl.multiple_of(qi * tq, tq), tq), :].T

    def scores(qslot, j, slot):
        rows = pl.ds(pl.multiple_of(j * tk, tk), tk)
        s = jnp.dot(k_ref[0, 0, rows, :], qt_scr[qslot], preferred_element_type=_F32)
        s_scr[slot] = s
        return jnp.max(s, axis=0, keepdims=True)

    def update(j, slot, m, acc, cmax):
        cols = pl.ds(pl.multiple_of(j * tk, tk), tk)
        m_new = jnp.maximum(m, cmax)
        p = jnp.exp2(s_scr[slot] - m_new).astype(_BF16)
        acc = jnp.exp2(m - m_new) * acc + jnp.dot(vt_ref[0, 0, :, cols], p,
                                                  preferred_element_type=_F32)
        return m_new, acc

    def tile(qi, c0):
        qslot = qi % 2
        load_qt(jnp.minimum(qi + 1, nq - 1), 1 - qslot)

        def body(jj, carry):
            m, acc, c0 = carry
            j = 2 * jj
            c1 = scores(qslot, j + 1, 1)
            m, acc = update(j, 0, m, acc, c0)
            wrap = j + 2 == n
            c0 = scores(jnp.where(wrap, 1 - qslot, qslot), jnp.where(wrap, 0, j + 2), 0)
            m, acc = update(j + 1, 1, m, acc, c1)
            return m, acc, c0

        carry = (jnp.full((1, tq), -jnp.inf, _F32), jnp.zeros((V_EXT, tq), _F32), c0)
        m, acc, c_next = lax.fori_loop(0, n // 2, body, carry, unroll=unroll)
        out = acc[:V_DIM] / acc[V_DIM:V_DIM + 1]
        o_ref[0, pl.ds(pl.multiple_of(qi * tq, tq), tq), :] = out.T.astype(o_ref.dtype)
        return c_next

    load_qt(0, 0)
    lax.fori_loop(0, nq, tile, scores(0, 0, 0))


MAX_UNROLL = 16
ATTN_SPAN = 8192


def _attention(q, k, vt, tq, tk, span):
    B, H, S, _ = q.shape
    pairs = S // tk // 2
    assert S % (2 * tk) == 0 and S % span == 0 and span % tq == 0
    unroll = max(u for u in range(1, MAX_UNROLL + 1) if pairs % u == 0)
    return pl.pallas_call(
        functools.partial(_attention_kernel, tq=tq, tk=tk, unroll=unroll),
        grid=(B, H, S // span),
        in_specs=[pl.BlockSpec((1, 1, span, HEAD_PAD), lambda b, h, i: (b, h, i, 0)),
                  pl.BlockSpec((1, 1, S, HEAD_PAD), lambda b, h, i: (b, h, 0, 0)),
                  pl.BlockSpec((1, 1, V_EXT, S), lambda b, h, i: (b, h, 0, 0))],
        out_specs=pl.BlockSpec((1, span, V_DIM), lambda b, h, i: (b, i, h)),
        out_shape=jax.ShapeDtypeStruct((B, S, ATTN_WIDTH), _BF16),
        scratch_shapes=[pltpu.VMEM((2, tk, tq), _F32), pltpu.VMEM((2, HEAD_PAD, tq), _BF16)],
        compiler_params=_params(("parallel", "parallel", "arbitrary")),
        name="attention",
    )(q, k, vt)


def _lru_conv(ux_ref, prev_ref, next_ref, cw_ref, cb_ref, *, tt, first, last):
    prev = jnp.where(first, 0.0, prev_ref[0])
    nxt = jnp.where(last, 0.0, next_ref[0])
    xe = jnp.concatenate([prev[SUBLANES - 1:], ux_ref[0], nxt[:2]], axis=0)
    cw = cw_ref[...]
    xc = cb_ref[...][None]
    for tap in range(CONV_W):
        xc = xc + xe[tap:tap + tt] * cw[tap][None]
    return xc


def _lru_gates(xc_scr, wg_ref, bg_ref, lam_ref, a_scr, b_scr, *, tt):
    k1 = LRU_C * jax.nn.softplus(-lam_ref[...])
    k2 = -LOG2_E * k1
    for n in range(LRU_BLOCKS):
        rows = pl.ds(n, tt, stride=LRU_BLOCKS)
        xn = xc_scr[rows, :]
        z = jnp.dot(xn.astype(_BF16), wg_ref[n], preferred_element_type=_F32) + bg_ref[n]
        sig = 0.5 * jnp.tanh(z) + 0.5
        r = sig[:, :LRU_BD]
        i = sig[:, LRU_BD:]
        a = jnp.exp2(r * k2[n])
        a_scr[rows, :] = a
        y = jnp.tanh(r * k1[n]) * (1.0 + a * a)
        b_scr[rows, :] = jnp.where(y > 0.0, y * lax.rsqrt(y), 0.0) * (i * xn)


SCAN_BLOCK = 8


def _scan_block(k, h, *, a_scr, b_scr, reverse, emit):
    rows = pl.ds(pl.multiple_of(k * SCAN_BLOCK * LRU_BLOCKS, SCAN_BLOCK * LRU_BLOCKS), SCAN_BLOCK * LRU_BLOCKS)
    a = a_scr[rows, :].reshape(SCAN_BLOCK, LRU_BLOCKS, LRU_BD)
    b = b_scr[rows, :].reshape(SCAN_BLOCK, LRU_BLOCKS, LRU_BD)
    order = range(SCAN_BLOCK - 1, -1, -1) if reverse else range(SCAN_BLOCK)
    hs = [None] * SCAN_BLOCK
    for t0, t1 in zip(order[0::2], order[1::2]):
        hs[t0] = a[t0] * h + b[t0]
        h = (a[t1] * a[t0]) * h + (a[t1] * b[t0] + b[t1])
        hs[t1] = h
    emit(k, jnp.stack(hs))
    return h


def _lru_fwd_kernel(ux_ref, prev_ref, next_ref, cw_ref, cb_ref, wg_ref, bg_ref, lam_ref,
                    hf_ref, xc_ref, xc_scr, a_scr, b_scr, h_scr, *, tt):
    i = pl.program_id(1)
    xc = _lru_conv(ux_ref, prev_ref, next_ref, cw_ref, cb_ref, tt=tt,
                   first=i == 0, last=i == pl.num_programs(1) - 1)
    xc_ref[0] = xc
    xc_scr[...] = xc.reshape(tt * LRU_BLOCKS, LRU_BD)
    _lru_gates(xc_scr, wg_ref, bg_ref, lam_ref, a_scr, b_scr, tt=tt)

    @pl.when(i == 0)
    def _():
        h_scr[...] = jnp.zeros(h_scr.shape, _F32)

    def emit(k, hs):
        hf_ref[0, pl.ds(k * SCAN_BLOCK, SCAN_BLOCK)] = hs

    h_scr[...] = lax.fori_loop(0, tt // SCAN_BLOCK, lambda k, h: _scan_block(
        k, h, a_scr=a_scr, b_scr=b_scr, reverse=False, emit=emit), h_scr[...])


def _lru_bwd_kernel(xc_ref, wg_ref, bg_ref, lam_ref, hf_ref, ug_ref,
                    out_ref, a_scr, b_scr, h_scr, g_scr, *, tt):
    _lru_gates(xc_ref.at[0], wg_ref, bg_ref, lam_ref, a_scr, b_scr, tt=tt)

    @pl.when(pl.program_id(1) == 0)
    def _():
        h_scr[...] = jnp.zeros(h_scr.shape, _F32)

    g_scr[...] = jax.nn.gelu(ug_ref[0])

    def emit(k, hs):
        toks = pl.ds(k * SCAN_BLOCK, SCAN_BLOCK)
        out_ref[0, toks] = (hf_ref[0, toks] + hs) * g_scr[toks]

    nb = tt // SCAN_BLOCK
    h_scr[...] = lax.fori_loop(0, nb, lambda i, h: _scan_block(
        nb - 1 - i, h, a_scr=a_scr, b_scr=b_scr, reverse=True, emit=emit), h_scr[...])


def _lru(ux, ug, conv_w, conv_b, wg, bg, lam, tt):
    B, S = ux.shape[:2]
    nt = S // tt
    nh = tt // SUBLANES
    tile4 = (1, tt, LRU_BLOCKS, LRU_BD)
    halo4 = (1, SUBLANES, LRU_BLOCKS, LRU_BD)
    rows3 = (1, tt * LRU_BLOCKS, LRU_BD)
    last_halo = S // SUBLANES - 1
    flat = lambda: pltpu.VMEM((tt * LRU_BLOCKS, LRU_BD), _F32)
    carry = pltpu.VMEM((LRU_BLOCKS, LRU_BD), _F32)
    conv_consts = [_const_spec((CONV_W, LRU_BLOCKS, LRU_BD)), _const_spec((LRU_BLOCKS, LRU_BD))]
    gate_consts = [_const_spec((LRU_BLOCKS, LRU_BD, 2 * LRU_BD)),
                   _const_spec((LRU_BLOCKS, 1, 2 * LRU_BD)), _const_spec((LRU_BLOCKS, 1, LRU_BD))]
    fwd_tile = pl.BlockSpec(tile4, lambda b, i: (b, i, 0, 0))
    bwd_tile = pl.BlockSpec(tile4, lambda b, i: (b, nt - 1 - i, 0, 0))
    shape4 = jax.ShapeDtypeStruct((B, S, LRU_BLOCKS, LRU_BD), _F32)
    hf, xc = pl.pallas_call(
        functools.partial(_lru_fwd_kernel, tt=tt),
        grid=(B, nt),
        in_specs=[fwd_tile,
                  pl.BlockSpec(halo4, lambda b, i: (b, jnp.maximum(i * nh - 1, 0), 0, 0)),
                  pl.BlockSpec(halo4, lambda b, i: (b, jnp.minimum((i + 1) * nh, last_halo), 0, 0))]
                 + conv_consts + gate_consts,
        out_specs=[fwd_tile, fwd_tile],
        out_shape=[shape4, shape4],
        scratch_shapes=[flat(), flat(), flat(), carry],
        compiler_params=_params(("parallel", "arbitrary")),
        name="lru_forward",
    )(ux, ux, ux, conv_w, conv_b, wg[0], bg[0], lam[0])
    return pl.pallas_call(
        functools.partial(_lru_bwd_kernel, tt=tt),
        grid=(B, nt),
        in_specs=[pl.BlockSpec(rows3, lambda b, i: (b, nt - 1 - i, 0))] + gate_consts + [bwd_tile, bwd_tile],
        out_specs=bwd_tile,
        out_shape=shape4,
        scratch_shapes=[flat(), flat(), carry, pltpu.VMEM((tt, LRU_BLOCKS, LRU_BD), _F32)],
        compiler_params=_params(("parallel", "arbitrary")),
        name="lru_backward",
    )(xc.reshape(B, S * LRU_BLOCKS, LRU_BD), wg[1], bg[1], lam[1], hf, ug)


def _out_proj_kernel(x_ref, attn_ref, lru_ref, ga_ref, gl_ref, wa_ref, wl_ref, y_ref):
    a = _rms(attn_ref[0].astype(_F32), ga_ref[...]).astype(_BF16)
    tm = x_ref.shape[1]
    lru = jnp.concatenate([lru_ref[0, pl.ds(n, tm, stride=LRU_BLOCKS), :] for n in range(LRU_BLOCKS)],
                          axis=1)
    l = _rms(lru, gl_ref[...]).astype(_BF16)
    y = jnp.dot(a, wa_ref[...], preferred_element_type=_F32)
    y = y + jnp.dot(l, wl_ref[...], preferred_element_type=_F32)
    y_ref[0] = x_ref[0] + y


def _out_proj(x, attn, lru, ga, gl, w_out_a, w_out_l, tm):
    B, S, _ = x.shape
    tok = lambda w: pl.BlockSpec((1, tm, w), lambda b, i: (b, i, 0))
    return pl.pallas_call(
        _out_proj_kernel,
        grid=(B, S // tm),
        in_specs=[tok(D_MODEL), tok(ATTN_WIDTH), _lru_rows_spec(tm),
                  _const_spec((1, ATTN_WIDTH)), _const_spec((1, LRU_WIDTH)),
                  _const_spec((ATTN_WIDTH, D_MODEL)), _const_spec((LRU_WIDTH, D_MODEL))],
        out_specs=tok(D_MODEL),
        out_shape=jax.ShapeDtypeStruct(x.shape, _F32),
        compiler_params=_params(("parallel", "parallel")),
        name="out_proj",
    )(x, attn, lru, ga, gl, w_out_a, w_out_l)


def _ffn_kernel(x_ref, g_ref, wg_ref, wu_ref, wd_ref, y_ref, h_scr):
    @pl.when(pl.program_id(2) == 0)
    def _():
        x = x_ref[0]
        h_scr[...] = _rms(x, g_ref[...]).astype(_BF16)
        y_ref[0] = x

    h = h_scr[...]
    gate = jnp.dot(h, wg_ref[...], preferred_element_type=_F32)
    up = jnp.dot(h, wu_ref[...], preferred_element_type=_F32)
    act = (jax.nn.silu(gate) * up).astype(_BF16)
    y_ref[0] += jnp.dot(act, wd_ref[...], preferred_element_type=_F32)


def _ffn(x, g, w_gate, w_up, w_down, tm, tf):
    B, S, _ = x.shape
    tok = pl.BlockSpec((1, tm, D_MODEL), lambda b, i, j: (b, i, 0))
    return pl.pallas_call(
        _ffn_kernel,
        grid=(B, S // tm, D_FF // tf),
        in_specs=[tok, pl.BlockSpec((1, D_MODEL), lambda b, i, j: (0, 0)),
                  pl.BlockSpec((D_MODEL, tf), lambda b, i, j: (0, j)),
                  pl.BlockSpec((D_MODEL, tf), lambda b, i, j: (0, j)),
                  pl.BlockSpec((tf, D_MODEL), lambda b, i, j: (j, 0))],
        out_specs=tok,
        out_shape=jax.ShapeDtypeStruct(x.shape, _F32),
        scratch_shapes=[pltpu.VMEM((tm, D_MODEL), _BF16)],
        compiler_params=_params(("parallel", "parallel", "arbitrary")),
        name="ffn",
    )(x, g, w_gate, w_up, w_down)


def _swap_halves(w):
    half = ROPE // 2
    return jnp.concatenate([-w[..., half:], w[..., :half]], axis=-1)


def _swap_gain(g):
    half = ROPE // 2
    return jnp.concatenate([g[half:], g[:half]])


def _gain_rows(g):
    pad = jnp.zeros((LANES - ROPE,), _F32)
    pe = g[NOPE:]
    return jnp.stack([g[:NOPE], jnp.concatenate([pe, pad]),
                      jnp.concatenate([_swap_gain(pe), pad])])


def _prepare(attn_norm_g, w_in, q_a_norm_g, w_uq, kv_a_norm_g, w_ukv, q_norm_g, k_norm_g,
             conv_w, conv_b, w_rg_r, b_rg_r, w_rg_i, b_rg_i, lru_lambda,
             attn_out_norm_g, lru_out_norm_g, w_out, ffn_norm_g, w_gate, w_up, w_down):
    k_pe = w_in[:, OFF_KV:OFF_KV + ROPE]
    w_head = jnp.concatenate([w_in[:, :OFF_KV + ROPE], _swap_halves(k_pe)], axis=1).astype(_BF16)
    w_lru = w_in[:, OFF_KV + ROPE:].astype(_BF16)
    wq = w_uq.reshape(Q_RANK, N_HEADS, QK_DIM)
    w_uq_ext = jnp.concatenate([wq, _swap_halves(wq[..., NOPE:])], axis=-1)
    w_uq_ext = w_uq_ext.reshape(Q_RANK, N_HEADS * HEAD_PAD).astype(_BF16)
    wkv = w_ukv.reshape(KV_RANK, N_HEADS, NOPE + V_DIM)
    w_uk = wkv[..., :NOPE].reshape(KV_RANK, N_HEADS * NOPE).astype(_BF16)
    w_uvt = wkv[..., NOPE:].reshape(KV_RANK, ATTN_WIDTH).T.astype(_BF16)
    wg = (0.5 * jnp.concatenate([w_rg_r, w_rg_i], axis=-1)).astype(_BF16)
    bg = 0.5 * jnp.concatenate([b_rg_r, b_rg_i], axis=-1)[:, :, None, :]
    lam = lru_lambda.reshape(2, LRU_BLOCKS, 1, LRU_BD)
    return dict(
        g_in=attn_norm_g[None], w_head=w_head, w_lru=w_lru, g_qa=q_a_norm_g[None], w_uq_ext=w_uq_ext,
        g_kva=kv_a_norm_g[None], w_uk=w_uk, w_uvt=w_uvt,
        gq=_gain_rows(q_norm_g), gk=_gain_rows(k_norm_g),
        conv_w=conv_w.reshape(CONV_W, LRU_BLOCKS, LRU_BD), conv_b=conv_b.reshape(LRU_BLOCKS, LRU_BD),
        wg=wg, bg=bg, lam=lam,
        ga=attn_out_norm_g[None], gl=lru_out_norm_g[None],
        w_out_a=w_out[:ATTN_WIDTH].astype(_BF16), w_out_l=w_out[ATTN_WIDTH:].astype(_BF16),
        g_ffn=ffn_norm_g[None], w_gate=w_gate.astype(_BF16), w_up=w_up.astype(_BF16),
        w_down=w_down.astype(_BF16))


def _rope_tables(S):
    half = ROPE // 2
    inv_freq = ROPE_BASE ** (-2.0 * np.arange(half, dtype=np.float64) / ROPE)
    ang = np.arange(S, dtype=np.float64)[:, None] * inv_freq[None, :]
    pad = np.zeros((S, LANES - ROPE))
    cos = np.concatenate([np.cos(ang), np.cos(ang), pad], axis=1)
    sin = np.concatenate([np.sin(ang), np.sin(ang), pad], axis=1)
    return jnp.asarray(cos, _F32), jnp.asarray(sin, _F32)


def _tiles(S):
    attn_k = min(512, S // 2)
    attn_q = min(1024 if S // attn_k <= 8 else 512, S)
    return dict(in_proj=min(512, S), attn_q=attn_q, attn_k=attn_k,
                attn_span=min(ATTN_SPAN, S),
                lru=min(1024, S), out_proj=min(512, S), ffn=min(512, S), ffn_cols=512)


def _encoder_layer(x, p):
    B, S, _ = x.shape
    t = _tiles(S)
    cos, sin = _rope_tables(S)
    q, k, vt, ux, ug = _in_proj(x, cos, sin, p["g_in"], p["w_head"], p["w_lru"], p["g_qa"], p["w_uq_ext"],
                               p["g_kva"], p["w_uk"], p["w_uvt"], p["gq"], p["gk"], t["in_proj"])
    attn = _attention(q, k, vt, t["attn_q"], t["attn_k"], t["attn_span"])
    shape4 = (B, S, LRU_BLOCKS, LRU_BD)
    lru = _lru(ux.reshape(shape4), ug.reshape(shape4), p["conv_w"], p["conv_b"],
               p["wg"], p["bg"], p["lam"], t["lru"]).reshape(B, S * LRU_BLOCKS, LRU_BD)
    x1 = _out_proj(x, attn, lru, p["ga"], p["gl"], p["w_out_a"], p["w_out_l"], t["out_proj"])
    return _ffn(x1, p["g_ffn"], p["w_gate"], p["w_up"], p["w_down"], t["ffn"], t["ffn_cols"])


def kernel(x_prompt, x_sample, attn_norm_g, w_in, q_a_norm_g, w_uq, kv_a_norm_g, w_ukv, q_norm_g, k_norm_g, conv_w, conv_b, w_rg_r, b_rg_r, w_rg_i, b_rg_i, lru_lambda, attn_out_norm_g, lru_out_norm_g, w_out, ffn_norm_g, w_gate, w_up, w_down):
    depth = attn_norm_g.shape[0]
    weights = (attn_norm_g, w_in, q_a_norm_g, w_uq, kv_a_norm_g, w_ukv, q_norm_g, k_norm_g,
               conv_w, conv_b, w_rg_r, b_rg_r, w_rg_i, b_rg_i, lru_lambda,
               attn_out_norm_g, lru_out_norm_g, w_out, ffn_norm_g, w_gate, w_up, w_down)
    y_prompt, y_sample = x_prompt, x_sample
    for layer in range(depth):
        p = _prepare(*(w[layer] for w in weights))
        y_prompt = _encoder_layer(y_prompt, p)
        y_sample = _encoder_layer(y_sample, p)
    return (y_prompt, y_sample)
```

```python
import functools

import jax
import jax.numpy as jnp
import numpy as np
from jax import lax
from jax.experimental import pallas as pl
from jax.experimental.pallas import tpu as pltpu

D_MODEL = 2048
N_HEADS = 8
NOPE = 128
ROPE = 64
QK_DIM = NOPE + ROPE
V_DIM = 128
BF16_SUBLANES = 16
V_EXT = V_DIM + BF16_SUBLANES
ATTN_WIDTH = N_HEADS * V_DIM
Q_RANK = 512
KV_RANK = 256
ROPE_BASE = 10000.0
LRU_WIDTH = D_MODEL - ATTN_WIDTH
LRU_BLOCKS = 8
LRU_BD = LRU_WIDTH // LRU_BLOCKS
CONV_W = 4
LRU_C = 8.0
D_FF = 5632
EPS = 1e-6
LOG2_E = 1.4426950408889634

LANES = 128
SUBLANES = 8
HEAD_PAD = 2 * LANES
OFF_Q = Q_RANK
OFF_KV = OFF_Q + KV_RANK
OFF_KG = OFF_KV + LANES
VMEM_LIMIT = 56 * 1024 * 1024

_F32 = jnp.float32
_BF16 = jnp.bfloat16


def _const_spec(shape):
    return pl.BlockSpec(shape, lambda *_: (0,) * len(shape), pipeline_mode=pl.Buffered(1))


def _params(sem):
    return pltpu.CompilerParams(dimension_semantics=sem, vmem_limit_bytes=VMEM_LIMIT)


def _rms(x, g):
    return x * lax.rsqrt(jnp.mean(x * x, axis=-1, keepdims=True) + EPS) * g


def _lru_rows_spec(tm):
    return pl.BlockSpec((1, tm * LRU_BLOCKS, LRU_BD), lambda b, i: (b, i, 0))


def _in_proj_kernel(x_ref, cos_ref, sin_ref, g_in_ref, w_head_ref, w_lru_ref, g_qa_ref, w_uq_ref,
                    g_kva_ref, w_uk_ref, w_uvt_ref, gq_ref, gk_ref,
                    q_ref, k_ref, vt_ref, ux_ref, ug_ref):
    x = x_ref[0]
    h = _rms(x, g_in_ref[...]).astype(_BF16)
    u = jnp.dot(h, w_head_ref[...], preferred_element_type=_F32)

    cos = cos_ref[...]
    sin = sin_ref[...]
    lane = lax.broadcasted_iota(jnp.int32, (1, LANES), 1)
    rope_mask = (lane < ROPE).astype(_F32)
    scale = QK_DIM ** -0.5 * LOG2_E

    def rotary(grp, g_rows):
        return grp * (g_rows[1:2] * cos) + pltpu.roll(grp, ROPE, axis=1) * (g_rows[2:3] * sin)

    def head_rms(nope, grp, extra=None):
        ss = jnp.sum(nope * nope, axis=-1, keepdims=True)
        ss = ss + (jnp.sum(grp * grp * rope_mask, axis=-1, keepdims=True) if extra is None else extra)
        return lax.rsqrt(ss * (1.0 / QK_DIM) + EPS)

    cq = _rms(u[:, :OFF_Q], g_qa_ref[...]).astype(_BF16)
    qf = jnp.dot(cq, w_uq_ref[...], preferred_element_type=_F32)
    gq = gq_ref[...]
    for hd in range(N_HEADS):
        nope = qf[:, hd * HEAD_PAD:hd * HEAD_PAD + NOPE]
        grp = qf[:, hd * HEAD_PAD + NOPE:(hd + 1) * HEAD_PAD]
        r = head_rms(nope, grp) * scale
        q_ref[0, hd, :, :NOPE] = (nope * r * gq[0:1]).astype(_BF16)
        q_ref[0, hd, :, NOPE:] = (rotary(grp, gq) * r).astype(_BF16)

    ckv = _rms(u[:, OFF_Q:OFF_KV], g_kva_ref[...]).astype(_BF16)
    vt = lax.dot_general(w_uvt_ref[...], ckv, (((1,), (1,)), ((), ())), preferred_element_type=_F32)
    ones_rows = (lax.broadcasted_iota(jnp.int32, (V_EXT - V_DIM, vt.shape[1]), 0) == 0).astype(_BF16)
    for hd in range(N_HEADS):
        vt_ref[0, hd, :V_DIM, :] = vt[hd * V_DIM:(hd + 1) * V_DIM].astype(_BF16)
        vt_ref[0, hd, V_DIM:, :] = ones_rows
    kn = jnp.dot(ckv, w_uk_ref[...], preferred_element_type=_F32)
    gk = gk_ref[...]
    kgrp = u[:, OFF_KV:OFF_KG]
    ss_pe = jnp.sum(kgrp * kgrp * rope_mask, axis=-1, keepdims=True)
    krot = rotary(kgrp, gk)
    for hd in range(N_HEADS):
        nope = kn[:, hd * NOPE:(hd + 1) * NOPE]
        r = head_rms(nope, None, ss_pe)
        k_ref[0, hd, :, :NOPE] = (nope * r * gk[0:1]).astype(_BF16)
        k_ref[0, hd, :, NOPE:] = (krot * r).astype(_BF16)

    ul = jnp.dot(h, w_lru_ref[...], preferred_element_type=_F32)
    for n in range(LRU_BLOCKS):
        rows = pl.ds(n, x.shape[0], stride=LRU_BLOCKS)
        ux_ref[0, rows, :] = ul[:, n * LRU_BD:(n + 1) * LRU_BD]
        ug_ref[0, rows, :] = ul[:, LRU_WIDTH + n * LRU_BD:LRU_WIDTH + (n + 1) * LRU_BD]


def _in_proj(x, cos, sin, g_in, w_head, w_lru, g_qa, w_uq_ext, g_kva, w_uk, w_uvt, gq, gk, tm):
    B, S, _ = x.shape
    grid = (B, S // tm)
    tok = lambda w: pl.BlockSpec((1, tm, w), lambda b, i: (b, i, 0))
    head = pl.BlockSpec((1, N_HEADS, tm, HEAD_PAD), lambda b, i: (b, 0, i, 0))
    tab = pl.BlockSpec((tm, LANES), lambda b, i: (i, 0))
    return pl.pallas_call(
        _in_proj_kernel,
        grid=grid,
        in_specs=[tok(D_MODEL), tab, tab,
                  _const_spec((1, D_MODEL)), _const_spec((D_MODEL, OFF_KG)), _const_spec((D_MODEL, 2 * LRU_WIDTH)),
                  _const_spec((1, Q_RANK)), _const_spec((Q_RANK, N_HEADS * HEAD_PAD)),
                  _const_spec((1, KV_RANK)), _const_spec((KV_RANK, N_HEADS * NOPE)),
                  _const_spec((ATTN_WIDTH, KV_RANK)),
                  _const_spec((3, LANES)), _const_spec((3, LANES))],
        out_specs=[head, head, pl.BlockSpec((1, N_HEADS, V_EXT, tm), lambda b, i: (b, 0, 0, i)),
                   _lru_rows_spec(tm), _lru_rows_spec(tm)],
        out_shape=[jax.ShapeDtypeStruct((B, N_HEADS, S, HEAD_PAD), _BF16),
                   jax.ShapeDtypeStruct((B, N_HEADS, S, HEAD_PAD), _BF16),
                   jax.ShapeDtypeStruct((B, N_HEADS, V_EXT, S), _BF16),
                   jax.ShapeDtypeStruct((B, S * LRU_BLOCKS, LRU_BD), _F32),
                   jax.ShapeDtypeStruct((B, S * LRU_BLOCKS, LRU_BD), _F32)],
        compiler_params=_params(("parallel", "parallel")),
        name="in_proj",
    )(x, cos, sin, g_in, w_head, w_lru, g_qa, w_uq_ext, g_kva, w_uk, w_uvt, gq, gk)


def _attention_kernel(q_ref, k_ref, vt_ref, o_ref, s_scr, qt_scr, *, tq, tk, unroll):
    n = k_ref.shape[2] // tk
    nq = q_ref.shape[2] // tq

    def load_qt(qi, slot):
        qt_scr[slot] = q_ref[0, 0, pl.ds(pl.multiple_of(qi * tq, tq), tq), :].T

    def scores(qslot, j, slot):
        rows = pl.ds(pl.multiple_of(j * tk, tk), tk)
        s = jnp.dot(k_ref[0, 0, rows, :], qt_scr[qslot], preferred_element_type=_F32)
        s_scr[slot] = s
        return jnp.max(s, axis=0, keepdims=True)

    def update(j, slot, m, acc, cmax):
        cols = pl.ds(pl.multiple_of(j * tk, tk), tk)
        m_new = jnp.maximum(m, cmax)
        p = jnp.exp2(s_scr[slot] - m_new).astype(_BF16)
        acc = jnp.exp2(m - m_new) * acc + jnp.dot(vt_ref[0, 0, :, cols], p,
                                                  preferred_element_type=_F32)
        return m_new, acc

    def tile(qi, c0):
        qslot = qi % 2
        load_qt(jnp.minimum(qi + 1, nq - 1), 1 - qslot)

        def body(jj, carry):
            m, acc, c0 = carry
            j = 2 * jj
            c1 = scores(qslot, j + 1, 1)
            m, acc = update(j, 0, m, acc, c0)
            wrap = j + 2 == n
            c0 = scores(jnp.where(wrap, 1 - qslot, qslot), jnp.where(wrap, 0, j + 2), 0)
            m, acc = update(j + 1, 1, m, acc, c1)
            return m, acc, c0

        carry = (jnp.full((1, tq), -jnp.inf, _F32), jnp.zeros((V_EXT, tq), _F32), c0)
        m, acc, c_next = lax.fori_loop(0, n // 2, body, carry, unroll=unroll)
        out = acc[:V_DIM] / acc[V_DIM:V_DIM + 1]
        o_ref[0, pl.ds(pl.multiple_of(qi * tq, tq), tq), :] = out.T.astype(o_ref.dtype)
        return c_next

    load_qt(0, 0)
    lax.fori_loop(0, nq, tile, scores(0, 0, 0), unroll=TILE_UNROLL if nq % TILE_UNROLL == 0 else 1)


MAX_UNROLL = 16
TILE_UNROLL = 2
ATTN_SPAN = 8192


def _attention(q, k, vt, tq, tk, span):
    B, H, S, _ = q.shape
    pairs = S // tk // 2
    assert S % (2 * tk) == 0 and S % span == 0 and span % tq == 0
    unroll = max(u for u in range(1, MAX_UNROLL + 1) if pairs % u == 0)
    return pl.pallas_call(
        functools.partial(_attention_kernel, tq=tq, tk=tk, unroll=unroll),
        grid=(B, H, S // span),
        in_specs=[pl.BlockSpec((1, 1, span, HEAD_PAD), lambda b, h, i: (b, h, i, 0)),
                  pl.BlockSpec((1, 1, S, HEAD_PAD), lambda b, h, i: (b, h, 0, 0)),
                  pl.BlockSpec((1, 1, V_EXT, S), lambda b, h, i: (b, h, 0, 0))],
        out_specs=pl.BlockSpec((1, span, V_DIM), lambda b, h, i: (b, i, h)),
        out_shape=jax.ShapeDtypeStruct((B, S, ATTN_WIDTH), _BF16),
        scratch_shapes=[pltpu.VMEM((2, tk, tq), _F32), pltpu.VMEM((2, HEAD_PAD, tq), _BF16)],
        compiler_params=_params(("parallel", "parallel", "arbitrary")),
        name="attention",
    )(q, k, vt)


def _lru_conv(ux_ref, prev_ref, next_ref, cw_ref, cb_ref, *, tt, first, last):
    prev = jnp.where(first, 0.0, prev_ref[0])
    nxt = jnp.where(last, 0.0, next_ref[0])
    xe = jnp.concatenate([prev[SUBLANES - 1:], ux_ref[0], nxt[:2]], axis=0)
    cw = cw_ref[...]
    xc = cb_ref[...][None]
    for tap in range(CONV_W):
        xc = xc + xe[tap:tap + tt] * cw[tap][None]
    return xc


def _lru_gates(xc_scr, wg_ref, bg_ref, lam_ref, a_scr, b_scr, *, tt):
    k1 = LRU_C * jax.nn.softplus(-lam_ref[...])
    k2 = -LOG2_E * k1
    for n in range(LRU_BLOCKS):
        rows = pl.ds(n, tt, stride=LRU_BLOCKS)
        xn = xc_scr[rows, :]
        z = jnp.dot(xn.astype(_BF16), wg_ref[n], preferred_element_type=_F32) + bg_ref[n]
        sig = 0.5 * jnp.tanh(z) + 0.5
        r = sig[:, :LRU_BD]
        i = sig[:, LRU_BD:]
        a = jnp.exp2(r * k2[n])
        a_scr[rows, :] = a
        y = jnp.tanh(r * k1[n]) * (1.0 + a * a)
        b_scr[rows, :] = jnp.where(y > 0.0, y * lax.rsqrt(y), 0.0) * (i * xn)


SCAN_BLOCK = 8


def _scan_block(k, h, *, a_scr, b_scr, reverse, emit):
    rows = pl.ds(pl.multiple_of(k * SCAN_BLOCK * LRU_BLOCKS, SCAN_BLOCK * LRU_BLOCKS), SCAN_BLOCK * LRU_BLOCKS)
    a = a_scr[rows, :].reshape(SCAN_BLOCK, LRU_BLOCKS, LRU_BD)
    b = b_scr[rows, :].reshape(SCAN_BLOCK, LRU_BLOCKS, LRU_BD)
    order = range(SCAN_BLOCK - 1, -1, -1) if reverse else range(SCAN_BLOCK)
    hs = [None] * SCAN_BLOCK
    for t0, t1 in zip(order[0::2], order[1::2]):
        hs[t0] = a[t0] * h + b[t0]
        h = (a[t1] * a[t0]) * h + (a[t1] * b[t0] + b[t1])
        hs[t1] = h
    emit(k, jnp.stack(hs))
    return h


def _lru_fwd_kernel(ux_ref, prev_ref, next_ref, cw_ref, cb_ref, wg_ref, bg_ref, lam_ref,
                    hf_ref, xc_ref, xc_scr, a_scr, b_scr, h_scr, *, tt):
    i = pl.program_id(1)
    xc = _lru_conv(ux_ref, prev_ref, next_ref, cw_ref, cb_ref, tt=tt,
                   first=i == 0, last=i == pl.num_programs(1) - 1)
    xc_ref[0] = xc
    xc_scr[...] = xc.reshape(tt * LRU_BLOCKS, LRU_BD)
    _lru_gates(xc_scr, wg_ref, bg_ref, lam_ref, a_scr, b_scr, tt=tt)

    @pl.when(i == 0)
    def _():
        h_scr[...] = jnp.zeros(h_scr.shape, _F32)

    def emit(k, hs):
        hf_ref[0, pl.ds(k * SCAN_BLOCK, SCAN_BLOCK)] = hs

    h_scr[...] = lax.fori_loop(0, tt // SCAN_BLOCK, lambda k, h: _scan_block(
        k, h, a_scr=a_scr, b_scr=b_scr, reverse=False, emit=emit), h_scr[...])


def _lru_bwd_kernel(xc_ref, wg_ref, bg_ref, lam_ref, hf_ref, ug_ref,
                    out_ref, a_scr, b_scr, h_scr, g_scr, *, tt):
    _lru_gates(xc_ref.at[0], wg_ref, bg_ref, lam_ref, a_scr, b_scr, tt=tt)

    @pl.when(pl.program_id(1) == 0)
    def _():
        h_scr[...] = jnp.zeros(h_scr.shape, _F32)

    g_scr[...] = jax.nn.gelu(ug_ref[0])

    def emit(k, hs):
        toks = pl.ds(k * SCAN_BLOCK, SCAN_BLOCK)
        out_ref[0, toks] = (hf_ref[0, toks] + hs) * g_scr[toks]

    nb = tt // SCAN_BLOCK
    h_scr[...] = lax.fori_loop(0, nb, lambda i, h: _scan_block(
        nb - 1 - i, h, a_scr=a_scr, b_scr=b_scr, reverse=True, emit=emit), h_scr[...])


def _lru(ux, ug, conv_w, conv_b, wg, bg, lam, tt):
    B, S = ux.shape[:2]
    nt = S // tt
    nh = tt // SUBLANES
    tile4 = (1, tt, LRU_BLOCKS, LRU_BD)
    halo4 = (1, SUBLANES, LRU_BLOCKS, LRU_BD)
    rows3 = (1, tt * LRU_BLOCKS, LRU_BD)
    last_halo = S // SUBLANES - 1
    flat = lambda: pltpu.VMEM((tt * LRU_BLOCKS, LRU_BD), _F32)
    carry = pltpu.VMEM((LRU_BLOCKS, LRU_BD), _F32)
    conv_consts = [_const_spec((CONV_W, LRU_BLOCKS, LRU_BD)), _const_spec((LRU_BLOCKS, LRU_BD))]
    gate_consts = [_const_spec((LRU_BLOCKS, LRU_BD, 2 * LRU_BD)),
                   _const_spec((LRU_BLOCKS, 1, 2 * LRU_BD)), _const_spec((LRU_BLOCKS, 1, LRU_BD))]
    fwd_tile = pl.BlockSpec(tile4, lambda b, i: (b, i, 0, 0))
    bwd_tile = pl.BlockSpec(tile4, lambda b, i: (b, nt - 1 - i, 0, 0))
    shape4 = jax.ShapeDtypeStruct((B, S, LRU_BLOCKS, LRU_BD), _F32)
    hf, xc = pl.pallas_call(
        functools.partial(_lru_fwd_kernel, tt=tt),
        grid=(B, nt),
        in_specs=[fwd_tile,
                  pl.BlockSpec(halo4, lambda b, i: (b, jnp.maximum(i * nh - 1, 0), 0, 0)),
                  pl.BlockSpec(halo4, lambda b, i: (b, jnp.minimum((i + 1) * nh, last_halo), 0, 0))]
                 + conv_consts + gate_consts,
        out_specs=[fwd_tile, fwd_tile],
        out_shape=[shape4, shape4],
        scratch_shapes=[flat(), flat(), flat(), carry],
        compiler_params=_params(("parallel", "arbitrary")),
        name="lru_forward",
    )(ux, ux, ux, conv_w, conv_b, wg[0], bg[0], lam[0])
    return pl.pallas_call(
        functools.partial(_lru_bwd_kernel, tt=tt),
        grid=(B, nt),
        in_specs=[pl.BlockSpec(rows3, lambda b, i: (b, nt - 1 - i, 0))] + gate_consts + [bwd_tile, bwd_tile],
        out_specs=bwd_tile,
        out_shape=shape4,
        scratch_shapes=[flat(), flat(), carry, pltpu.VMEM((tt, LRU_BLOCKS, LRU_BD), _F32)],
        compiler_params=_params(("parallel", "arbitrary")),
        name="lru_backward",
    )(xc.reshape(B, S * LRU_BLOCKS, LRU_BD), wg[1], bg[1], lam[1], hf, ug)


def _out_proj_kernel(x_ref, attn_ref, lru_ref, ga_ref, gl_ref, wa_ref, wl_ref, y_ref):
    a = _rms(attn_ref[0].astype(_F32), ga_ref[...]).astype(_BF16)
    tm = x_ref.shape[1]
    lru = jnp.concatenate([lru_ref[0, pl.ds(n, tm, stride=LRU_BLOCKS), :] for n in range(LRU_BLOCKS)],
                          axis=1)
    l = _rms(lru, gl_ref[...]).astype(_BF16)
    y = jnp.dot(a, wa_ref[...], preferred_element_type=_F32)
    y = y + jnp.dot(l, wl_ref[...], preferred_element_type=_F32)
    y_ref[0] = x_ref[0] + y


def _out_proj(x, attn, lru, ga, gl, w_out_a, w_out_l, tm):
    B, S, _ = x.shape
    tok = lambda w: pl.BlockSpec((1, tm, w), lambda b, i: (b, i, 0))
    return pl.pallas_call(
        _out_proj_kernel,
        grid=(B, S // tm),
        in_specs=[tok(D_MODEL), tok(ATTN_WIDTH), _lru_rows_spec(tm),
                  _const_spec((1, ATTN_WIDTH)), _const_spec((1, LRU_WIDTH)),
                  _const_spec((ATTN_WIDTH, D_MODEL)), _const_spec((LRU_WIDTH, D_MODEL))],
        out_specs=tok(D_MODEL),
        out_shape=jax.ShapeDtypeStruct(x.shape, _F32),
        compiler_params=_params(("parallel", "parallel")),
        name="out_proj",
    )(x, attn, lru, ga, gl, w_out_a, w_out_l)


def _ffn_kernel(x_ref, g_ref, wg_ref, wu_ref, wd_ref, y_ref, h_scr):
    @pl.when(pl.program_id(2) == 0)
    def _():
        x = x_ref[0]
        h_scr[...] = _rms(x, g_ref[...]).astype(_BF16)
        y_ref[0] = x

    h = h_scr[...]
    gate = jnp.dot(h, wg_ref[...], preferred_element_type=_F32)
    up = jnp.dot(h, wu_ref[...], preferred_element_type=_F32)
    act = (jax.nn.silu(gate) * up).astype(_BF16)
    y_ref[0] += jnp.dot(act, wd_ref[...], preferred_element_type=_F32)


def _ffn(x, g, w_gate, w_up, w_down, tm, tf):
    B, S, _ = x.shape
    tok = pl.BlockSpec((1, tm, D_MODEL), lambda b, i, j: (b, i, 0))
    return pl.pallas_call(
        _ffn_kernel,
        grid=(B, S // tm, D_FF // tf),
        in_specs=[tok, pl.BlockSpec((1, D_MODEL), lambda b, i, j: (0, 0)),
                  pl.BlockSpec((D_MODEL, tf), lambda b, i, j: (0, j)),
                  pl.BlockSpec((D_MODEL, tf), lambda b, i, j: (0, j)),
                  pl.BlockSpec((tf, D_MODEL), lambda b, i, j: (j, 0))],
        out_specs=tok,
        out_shape=jax.ShapeDtypeStruct(x.shape, _F32),
        scratch_shapes=[pltpu.VMEM((tm, D_MODEL), _BF16)],
        compiler_params=_params(("parallel", "parallel", "arbitrary")),
        name="ffn",
    )(x, g, w_gate, w_up, w_down)


def _swap_halves(w):
    half = ROPE // 2
    return jnp.concatenate([-w[..., half:], w[..., :half]], axis=-1)


def _swap_gain(g):
    half = ROPE // 2
    return jnp.concatenate([g[half:], g[:half]])


def _gain_rows(g):
    pad = jnp.zeros((LANES - ROPE,), _F32)
    pe = g[NOPE:]
    return jnp.stack([g[:NOPE], jnp.concatenate([pe, pad]),
                      jnp.concatenate([_swap_gain(pe), pad])])


def _prepare(attn_norm_g, w_in, q_a_norm_g, w_uq, kv_a_norm_g, w_ukv, q_norm_g, k_norm_g,
             conv_w, conv_b, w_rg_r, b_rg_r, w_rg_i, b_rg_i, lru_lambda,
             attn_out_norm_g, lru_out_norm_g, w_out, ffn_norm_g, w_gate, w_up, w_down):
    k_pe = w_in[:, OFF_KV:OFF_KV + ROPE]
    w_head = jnp.concatenate([w_in[:, :OFF_KV + ROPE], _swap_halves(k_pe)], axis=1).astype(_BF16)
    w_lru = w_in[:, OFF_KV + ROPE:].astype(_BF16)
    wq = w_uq.reshape(Q_RANK, N_HEADS, QK_DIM)
    w_uq_ext = jnp.concatenate([wq, _swap_halves(wq[..., NOPE:])], axis=-1)
    w_uq_ext = w_uq_ext.reshape(Q_RANK, N_HEADS * HEAD_PAD).astype(_BF16)
    wkv = w_ukv.reshape(KV_RANK, N_HEADS, NOPE + V_DIM)
    w_uk = wkv[..., :NOPE].reshape(KV_RANK, N_HEADS * NOPE).astype(_BF16)
    w_uvt = wkv[..., NOPE:].reshape(KV_RANK, ATTN_WIDTH).T.astype(_BF16)
    wg = (0.5 * jnp.concatenate([w_rg_r, w_rg_i], axis=-1)).astype(_BF16)
    bg = 0.5 * jnp.concatenate([b_rg_r, b_rg_i], axis=-1)[:, :, None, :]
    lam = lru_lambda.reshape(2, LRU_BLOCKS, 1, LRU_BD)
    return dict(
        g_in=attn_norm_g[None], w_head=w_head, w_lru=w_lru, g_qa=q_a_norm_g[None], w_uq_ext=w_uq_ext,
        g_kva=kv_a_norm_g[None], w_uk=w_uk, w_uvt=w_uvt,
        gq=_gain_rows(q_norm_g), gk=_gain_rows(k_norm_g),
        conv_w=conv_w.reshape(CONV_W, LRU_BLOCKS, LRU_BD), conv_b=conv_b.reshape(LRU_BLOCKS, LRU_BD),
        wg=wg, bg=bg, lam=lam,
        ga=attn_out_norm_g[None], gl=lru_out_norm_g[None],
        w_out_a=w_out[:ATTN_WIDTH].astype(_BF16), w_out_l=w_out[ATTN_WIDTH:].astype(_BF16),
        g_ffn=ffn_norm_g[None], w_gate=w_gate.astype(_BF16), w_up=w_up.astype(_BF16),
        w_down=w_down.astype(_BF16))


def _rope_tables(S):
    half = ROPE // 2
    inv_freq = ROPE_BASE ** (-2.0 * np.arange(half, dtype=np.float64) / ROPE)
    ang = np.arange(S, dtype=np.float64)[:, None] * inv_freq[None, :]
    pad = np.zeros((S, LANES - ROPE))
    cos = np.concatenate([np.cos(ang), np.cos(ang), pad], axis=1)
    sin = np.concatenate([np.sin(ang), np.sin(ang), pad], axis=1)
    return jnp.asarray(cos, _F32), jnp.asarray(sin, _F32)


def _tiles(S):
    attn_k = min(512, S // 2)
    attn_q = min(1024 if S // attn_k <= 8 else 512, S)
    return dict(in_proj=min(512, S), attn_q=attn_q, attn_k=attn_k,
                attn_span=min(ATTN_SPAN, S),
                lru=min(1024, S), out_proj=min(512, S), ffn=min(512, S), ffn_cols=512)


def _encoder_layer(x, p):
    B, S, _ = x.shape
    t = _tiles(S)
    cos, sin = _rope_tables(S)
    q, k, vt, ux, ug = _in_proj(x, cos, sin, p["g_in"], p["w_head"], p["w_lru"], p["g_qa"], p["w_uq_ext"],
                               p["g_kva"], p["w_uk"], p["w_uvt"], p["gq"], p["gk"], t["in_proj"])
    attn = _attention(q, k, vt, t["attn_q"], t["attn_k"], t["attn_span"])
    shape4 = (B, S, LRU_BLOCKS, LRU_BD)
    lru = _lru(ux.reshape(shape4), ug.reshape(shape4), p["conv_w"], p["conv_b"],
               p["wg"], p["bg"], p["lam"], t["lru"]).reshape(B, S * LRU_BLOCKS, LRU_BD)
    x1 = _out_proj(x, attn, lru, p["ga"], p["gl"], p["w_out_a"], p["w_out_l"], t["out_proj"])
    return _ffn(x1, p["g_ffn"], p["w_gate"], p["w_up"], p["w_down"], t["ffn"], t["ffn_cols"])


def kernel(x_prompt, x_sample, attn_norm_g, w_in, q_a_norm_g, w_uq, kv_a_norm_g, w_ukv, q_norm_g, k_norm_g, conv_w, conv_b, w_rg_r, b_rg_r, w_rg_i, b_rg_i, lru_lambda, attn_out_norm_g, lru_out_norm_g, w_out, ffn_norm_g, w_gate, w_up, w_down):
    depth = attn_norm_g.shape[0]
    weights = (attn_norm_g, w_in, q_a_norm_g, w_uq, kv_a_norm_g, w_ukv, q_norm_g, k_norm_g,
               conv_w, conv_b, w_rg_r, b_rg_r, w_rg_i, b_rg_i, lru_lambda,
               attn_out_norm_g, lru_out_norm_g, w_out, ffn_norm_g, w_gate, w_up, w_down)
    y_prompt, y_sample = x_prompt, x_sample
    for layer in range(depth):
        p = _prepare(*(w[layer] for w in weights))
        y_prompt = _encoder_layer(y_prompt, p)
        y_sample = _encoder_layer(y_sample, p)
    return (y_prompt, y_sample)
```

```python
import functools

import jax
import jax.numpy as jnp
import numpy as np
from jax import lax
from jax.experimental import pallas as pl
from jax.experimental.pallas import tpu as pltpu

D_MODEL = 2048
N_HEADS = 8
NOPE = 128
ROPE = 64
QK_DIM = NOPE + ROPE
V_DIM = 128
BF16_SUBLANES = 16
V_EXT = V_DIM + BF16_SUBLANES
ATTN_WIDTH = N_HEADS * V_DIM
Q_RANK = 512
KV_RANK = 256
ROPE_BASE = 10000.0
LRU_WIDTH = D_MODEL - ATTN_WIDTH
LRU_BLOCKS = 8
LRU_BD = LRU_WIDTH // LRU_BLOCKS
CONV_W = 4
LRU_C = 8.0
D_FF = 5632
EPS = 1e-6
LOG2_E = 1.4426950408889634

LANES = 128
SUBLANES = 8
HEAD_PAD = 2 * LANES
OFF_Q = Q_RANK
OFF_KV = OFF_Q + KV_RANK
OFF_KG = OFF_KV + LANES
VMEM_LIMIT = 56 * 1024 * 1024

_F32 = jnp.float32
_BF16 = jnp.bfloat16


def _const_spec(shape):
    return pl.BlockSpec(shape, lambda *_: (0,) * len(shape), pipeline_mode=pl.Buffered(1))


def _params(sem):
    return pltpu.CompilerParams(dimension_semantics=sem, vmem_limit_bytes=VMEM_LIMIT)


def _rms(x, g):
    return x * lax.rsqrt(jnp.mean(x * x, axis=-1, keepdims=True) + EPS) * g


def _lru_rows_spec(tm):
    return pl.BlockSpec((1, tm * LRU_BLOCKS, LRU_BD), lambda b, i: (b, i, 0))


def _in_proj_kernel(x_ref, cos_ref, sin_ref, g_in_ref, w_head_ref, w_lru_ref, g_qa_ref, w_uq_ref,
                    g_kva_ref, w_uk_ref, w_uvt_ref, gq_ref, gk_ref,
                    q_ref, k_ref, vt_ref, ux_ref, ug_ref):
    x = x_ref[0]
    h = _rms(x, g_in_ref[...]).astype(_BF16)
    u = jnp.dot(h, w_head_ref[...], preferred_element_type=_F32)

    cos = cos_ref[...]
    sin = sin_ref[...]
    lane = lax.broadcasted_iota(jnp.int32, (1, LANES), 1)
    rope_mask = (lane < ROPE).astype(_F32)
    scale = QK_DIM ** -0.5 * LOG2_E

    def rotary(grp, g_rows):
        return grp * (g_rows[1:2] * cos) + pltpu.roll(grp, ROPE, axis=1) * (g_rows[2:3] * sin)

    def head_rms(nope, grp, extra=None):
        ss = jnp.sum(nope * nope, axis=-1, keepdims=True)
        ss = ss + (jnp.sum(grp * grp * rope_mask, axis=-1, keepdims=True) if extra is None else extra)
        return lax.rsqrt(ss * (1.0 / QK_DIM) + EPS)

    cq = _rms(u[:, :OFF_Q], g_qa_ref[...]).astype(_BF16)
    qf = jnp.dot(cq, w_uq_ref[...], preferred_element_type=_F32)
    gq = gq_ref[...]
    for hd in range(N_HEADS):
        nope = qf[:, hd * HEAD_PAD:hd * HEAD_PAD + NOPE]
        grp = qf[:, hd * HEAD_PAD + NOPE:(hd + 1) * HEAD_PAD]
        r = head_rms(nope, grp) * scale
        q_ref[0, hd, :, :NOPE] = (nope * r * gq[0:1]).astype(_BF16)
        q_ref[0, hd, :, NOPE:] = (rotary(grp, gq) * r).astype(_BF16)

    ckv = _rms(u[:, OFF_Q:OFF_KV], g_kva_ref[...]).astype(_BF16)
    vt = lax.dot_general(w_uvt_ref[...], ckv, (((1,), (1,)), ((), ())), preferred_element_type=_F32)
    ones_rows = (lax.broadcasted_iota(jnp.int32, (V_EXT - V_DIM, vt.shape[1]), 0) == 0).astype(_BF16)
    for hd in range(N_HEADS):
        vt_ref[0, hd, :V_DIM, :] = vt[hd * V_DIM:(hd + 1) * V_DIM].astype(_BF16)
        vt_ref[0, hd, V_DIM:, :] = ones_rows
    kn = jnp.dot(ckv, w_uk_ref[...], preferred_element_type=_F32)
    gk = gk_ref[...]
    kgrp = u[:, OFF_KV:OFF_KG]
    ss_pe = jnp.sum(kgrp * kgrp * rope_mask, axis=-1, keepdims=True)
    krot = rotary(kgrp, gk)
    for hd in range(N_HEADS):
        nope = kn[:, hd * NOPE:(hd + 1) * NOPE]
        r = head_rms(nope, None, ss_pe)
        k_ref[0, hd, :, :NOPE] = (nope * r * gk[0:1]).astype(_BF16)
        k_ref[0, hd, :, NOPE:] = (krot * r).astype(_BF16)

    ul = jnp.dot(h, w_lru_ref[...], preferred_element_type=_F32)
    for n in range(LRU_BLOCKS):
        rows = pl.ds(n, x.shape[0], stride=LRU_BLOCKS)
        ux_ref[0, rows, :] = ul[:, n * LRU_BD:(n + 1) * LRU_BD]
        ug_ref[0, rows, :] = ul[:, LRU_WIDTH + n * LRU_BD:LRU_WIDTH + (n + 1) * LRU_BD]


def _in_proj(x, cos, sin, g_in, w_head, w_lru, g_qa, w_uq_ext, g_kva, w_uk, w_uvt, gq, gk, tm):
    B, S, _ = x.shape
    grid = (B, S // tm)
    tok = lambda w: pl.BlockSpec((1, tm, w), lambda b, i: (b, i, 0))
    head = pl.BlockSpec((1, N_HEADS, tm, HEAD_PAD), lambda b, i: (b, 0, i, 0))
    tab = pl.BlockSpec((tm, LANES), lambda b, i: (i, 0))
    return pl.pallas_call(
        _in_proj_kernel,
        grid=grid,
        in_specs=[tok(D_MODEL), tab, tab,
                  _const_spec((1, D_MODEL)), _const_spec((D_MODEL, OFF_KG)), _const_spec((D_MODEL, 2 * LRU_WIDTH)),
                  _const_spec((1, Q_RANK)), _const_spec((Q_RANK, N_HEADS * HEAD_PAD)),
                  _const_spec((1, KV_RANK)), _const_spec((KV_RANK, N_HEADS * NOPE)),
                  _const_spec((ATTN_WIDTH, KV_RANK)),
                  _const_spec((3, LANES)), _const_spec((3, LANES))],
        out_specs=[head, head, pl.BlockSpec((1, N_HEADS, V_EXT, tm), lambda b, i: (b, 0, 0, i)),
                   _lru_rows_spec(tm), _lru_rows_spec(tm)],
        out_shape=[jax.ShapeDtypeStruct((B, N_HEADS, S, HEAD_PAD), _BF16),
                   jax.ShapeDtypeStruct((B, N_HEADS, S, HEAD_PAD), _BF16),
                   jax.ShapeDtypeStruct((B, N_HEADS, V_EXT, S), _BF16),
                   jax.ShapeDtypeStruct((B, S * LRU_BLOCKS, LRU_BD), _F32),
                   jax.ShapeDtypeStruct((B, S * LRU_BLOCKS, LRU_BD), _F32)],
        compiler_params=_params(("parallel", "parallel")),
        name="in_proj",
    )(x, cos, sin, g_in, w_head, w_lru, g_qa, w_uq_ext, g_kva, w_uk, w_uvt, gq, gk)


def _attention_kernel(q_ref, k_ref, vt_ref, o_ref, s_scr, qt_scr, *, tq, tk, unroll):
    n = k_ref.shape[2] // tk
    nq = q_ref.shape[2] // tq

    def load_qt(qi, slot):
        qt_scr[slot] = q_ref[0, 0, pl.ds(pl.multiple_of(qi * tq, tq), tq), :].T

    def scores(qslot, j, slot):
        rows = pl.ds(pl.multiple_of(j * tk, tk), tk)
        s = jnp.dot(k_ref[0, 0, rows, :], qt_scr[qslot], preferred_element_type=_F32)
        s_scr[slot] = s
        return jnp.max(s, axis=0, keepdims=True)

    def update(j, slot, m, acc, cmax):
        cols = pl.ds(pl.multiple_of(j * tk, tk), tk)
        m_new = jnp.maximum(m, cmax)
        p = jnp.exp2(s_scr[slot] - m_new).astype(_BF16)
        acc = jnp.exp2(m - m_new) * acc + jnp.dot(vt_ref[0, 0, :, cols], p,
                                                  preferred_element_type=_F32)
        return m_new, acc

    def tile(qi, c0):
        qslot = qi % 2
        load_qt(jnp.minimum(qi + 1, nq - 1), 1 - qslot)

        def body(jj, carry):
            m, acc, c0 = carry
            j = 2 * jj
            c1 = scores(qslot, j + 1, 1)
            m, acc = update(j, 0, m, acc, c0)
            wrap = j + 2 == n
            c0 = scores(jnp.where(wrap, 1 - qslot, qslot), jnp.where(wrap, 0, j + 2), 0)
            m, acc = update(j + 1, 1, m, acc, c1)
            return m, acc, c0

        carry = (jnp.full((1, tq), -jnp.inf, _F32), jnp.zeros((V_EXT, tq), _F32), c0)
        m, acc, c_next = lax.fori_loop(0, n // 2, body, carry, unroll=unroll)
        out = acc[:V_DIM] / acc[V_DIM:V_DIM + 1]
        o_ref[0, pl.ds(pl.multiple_of(qi * tq, tq), tq), :] = out.T.astype(o_ref.dtype)
        return c_next

    load_qt(0, 0)
    lax.fori_loop(0, nq, tile, scores(0, 0, 0), unroll=TILE_UNROLL if nq % TILE_UNROLL == 0 else 1)


MAX_UNROLL = 16
TILE_UNROLL = 2
ATTN_SPAN = 8192


def _attention(q, k, vt, tq, tk, span):
    B, H, S, _ = q.shape
    pairs = S // tk // 2
    assert S % (2 * tk) == 0 and S % span == 0 and span % tq == 0
    unroll = max(u for u in range(1, MAX_UNROLL + 1) if pairs % u == 0)
    return pl.pallas_call(
        functools.partial(_attention_kernel, tq=tq, tk=tk, unroll=unroll),
        grid=(B, H, S // span),
        in_specs=[pl.BlockSpec((1, 1, span, HEAD_PAD), lambda b, h, i: (b, h, i, 0)),
                  pl.BlockSpec((1, 1, S, HEAD_PAD), lambda b, h, i: (b, h, 0, 0)),
                  pl.BlockSpec((1, 1, V_EXT, S), lambda b, h, i: (b, h, 0, 0))],
        out_specs=pl.BlockSpec((1, span, V_DIM), lambda b, h, i: (b, i, h)),
        out_shape=jax.ShapeDtypeStruct((B, S, ATTN_WIDTH), _BF16),
        scratch_shapes=[pltpu.VMEM((2, tk, tq), _F32), pltpu.VMEM((2, HEAD_PAD, tq), _BF16)],
        compiler_params=_params(("parallel", "parallel", "arbitrary")),
        name="attention",
    )(q, k, vt)


def _lru_conv(ux_ref, prev_ref, next_ref, cw_ref, cb_ref, *, tt, first, last):
    prev = jnp.where(first, 0.0, prev_ref[0])
    nxt = jnp.where(last, 0.0, next_ref[0])
    xe = jnp.concatenate([prev[SUBLANES - 1:], ux_ref[0], nxt[:2]], axis=0)
    cw = cw_ref[...]
    xc = cb_ref[...][None]
    for tap in range(CONV_W):
        xc = xc + xe[tap:tap + tt] * cw[tap][None]
    return xc


def _lru_gates(xc_scr, wg_ref, bg_ref, lam_ref, a_scr, b_scr, *, tt):
    k1 = LRU_C * jax.nn.softplus(-lam_ref[...])
    k2 = -LOG2_E * k1
    for n in range(LRU_BLOCKS):
        rows = pl.ds(n, tt, stride=LRU_BLOCKS)
        xn = xc_scr[rows, :]
        z = jnp.dot(xn.astype(_BF16), wg_ref[n], preferred_element_type=_F32) + bg_ref[n]
        sig = 0.5 * jnp.tanh(z) + 0.5
        r = sig[:, :LRU_BD]
        i = sig[:, LRU_BD:]
        a = jnp.exp2(r * k2[n])
        a_scr[rows, :] = a
        y = jnp.tanh(r * k1[n]) * (1.0 + a * a)
        b_scr[rows, :] = jnp.where(y > 0.0, y * lax.rsqrt(y), 0.0) * (i * xn)


SCAN_BLOCK = 8


def _scan_block(k, h, *, a_scr, b_scr, reverse, emit):
    rows = pl.ds(pl.multiple_of(k * SCAN_BLOCK * LRU_BLOCKS, SCAN_BLOCK * LRU_BLOCKS), SCAN_BLOCK * LRU_BLOCKS)
    a = a_scr[rows, :].reshape(SCAN_BLOCK, LRU_BLOCKS, LRU_BD)
    b = b_scr[rows, :].reshape(SCAN_BLOCK, LRU_BLOCKS, LRU_BD)
    order = range(SCAN_BLOCK - 1, -1, -1) if reverse else range(SCAN_BLOCK)
    hs = [None] * SCAN_BLOCK
    for t0, t1 in zip(order[0::2], order[1::2]):
        hs[t0] = a[t0] * h + b[t0]
        h = (a[t1] * a[t0]) * h + (a[t1] * b[t0] + b[t1])
        hs[t1] = h
    emit(k, jnp.stack(hs))
    return h


def _lru_fwd_kernel(ux_ref, prev_ref, next_ref, cw_ref, cb_ref, wg_ref, bg_ref, lam_ref,
                    hf_ref, xc_ref, xc_scr, a_scr, b_scr, h_scr, *, tt):
    i = pl.program_id(1)
    xc = _lru_conv(ux_ref, prev_ref, next_ref, cw_ref, cb_ref, tt=tt,
                   first=i == 0, last=i == pl.num_programs(1) - 1)
    xc_ref[0] = xc
    xc_scr[...] = xc.reshape(tt * LRU_BLOCKS, LRU_BD)
    _lru_gates(xc_scr, wg_ref, bg_ref, lam_ref, a_scr, b_scr, tt=tt)

    @pl.when(i == 0)
    def _():
        h_scr[...] = jnp.zeros(h_scr.shape, _F32)

    def emit(k, hs):
        hf_ref[0, pl.ds(k * SCAN_BLOCK, SCAN_BLOCK)] = hs

    h_scr[...] = lax.fori_loop(0, tt // SCAN_BLOCK, lambda k, h: _scan_block(
        k, h, a_scr=a_scr, b_scr=b_scr, reverse=False, emit=emit), h_scr[...])


def _lru_bwd_kernel(xc_ref, wg_ref, bg_ref, lam_ref, hf_ref, ug_ref,
                    out_ref, a_scr, b_scr, h_scr, g_scr, *, tt):
    _lru_gates(xc_ref.at[0], wg_ref, bg_ref, lam_ref, a_scr, b_scr, tt=tt)

    @pl.when(pl.program_id(1) == 0)
    def _():
        h_scr[...] = jnp.zeros(h_scr.shape, _F32)

    g_scr[...] = jax.nn.gelu(ug_ref[0])

    def emit(k, hs):
        toks = pl.ds(k * SCAN_BLOCK, SCAN_BLOCK)
        out_ref[0, toks] = (hf_ref[0, toks] + hs) * g_scr[toks]

    nb = tt // SCAN_BLOCK
    h_scr[...] = lax.fori_loop(0, nb, lambda i, h: _scan_block(
        nb - 1 - i, h, a_scr=a_scr, b_scr=b_scr, reverse=True, emit=emit), h_scr[...])


def _lru(ux, ug, conv_w, conv_b, wg, bg, lam, tt):
    B, S = ux.shape[:2]
    nt = S // tt
    nh = tt // SUBLANES
    tile4 = (1, tt, LRU_BLOCKS, LRU_BD)
    halo4 = (1, SUBLANES, LRU_BLOCKS, LRU_BD)
    rows3 = (1, tt * LRU_BLOCKS, LRU_BD)
    last_halo = S // SUBLANES - 1
    flat = lambda: pltpu.VMEM((tt * LRU_BLOCKS, LRU_BD), _F32)
    carry = pltpu.VMEM((LRU_BLOCKS, LRU_BD), _F32)
    conv_consts = [_const_spec((CONV_W, LRU_BLOCKS, LRU_BD)), _const_spec((LRU_BLOCKS, LRU_BD))]
    gate_consts = [_const_spec((LRU_BLOCKS, LRU_BD, 2 * LRU_BD)),
                   _const_spec((LRU_BLOCKS, 1, 2 * LRU_BD)), _const_spec((LRU_BLOCKS, 1, LRU_BD))]
    fwd_tile = pl.BlockSpec(tile4, lambda b, i: (b, i, 0, 0))
    bwd_tile = pl.BlockSpec(tile4, lambda b, i: (b, nt - 1 - i, 0, 0))
    shape4 = jax.ShapeDtypeStruct((B, S, LRU_BLOCKS, LRU_BD), _F32)
    hf, xc = pl.pallas_call(
        functools.partial(_lru_fwd_kernel, tt=tt),
        grid=(B, nt),
        in_specs=[fwd_tile,
                  pl.BlockSpec(halo4, lambda b, i: (b, jnp.maximum(i * nh - 1, 0), 0, 0)),
                  pl.BlockSpec(halo4, lambda b, i: (b, jnp.minimum((i + 1) * nh, last_halo), 0, 0))]
                 + conv_consts + gate_consts,
        out_specs=[fwd_tile, fwd_tile],
        out_shape=[shape4, shape4],
        scratch_shapes=[flat(), flat(), flat(), carry],
        compiler_params=_params(("parallel", "arbitrary")),
        name="lru_forward",
    )(ux, ux, ux, conv_w, conv_b, wg[0], bg[0], lam[0])
    return pl.pallas_call(
        functools.partial(_lru_bwd_kernel, tt=tt),
        grid=(B, nt),
        in_specs=[pl.BlockSpec(rows3, lambda b, i: (b, nt - 1 - i, 0))] + gate_consts + [bwd_tile, bwd_tile],
        out_specs=bwd_tile,
        out_shape=shape4,
        scratch_shapes=[flat(), flat(), carry, pltpu.VMEM((tt, LRU_BLOCKS, LRU_BD), _F32)],
        compiler_params=_params(("parallel", "arbitrary")),
        name="lru_backward",
    )(xc.reshape(B, S * LRU_BLOCKS, LRU_BD), wg[1], bg[1], lam[1], hf, ug)


def _out_proj_kernel(x_ref, attn_ref, lru_ref, ga_ref, gl_ref, wa_ref, wl_ref, y_ref):
    a = _rms(attn_ref[0].astype(_F32), ga_ref[...]).astype(_BF16)
    tm = x_ref.shape[1]
    lru = jnp.concatenate([lru_ref[0, pl.ds(n, tm, stride=LRU_BLOCKS), :] for n in range(LRU_BLOCKS)],
                          axis=1)
    l = _rms(lru, gl_ref[...]).astype(_BF16)
    y = jnp.dot(a, wa_ref[...], preferred_element_type=_F32)
    y = y + jnp.dot(l, wl_ref[...], preferred_element_type=_F32)
    y_ref[0] = x_ref[0] + y


def _out_proj(x, attn, lru, ga, gl, w_out_a, w_out_l, tm):
    B, S, _ = x.shape
    tok = lambda w: pl.BlockSpec((1, tm, w), lambda b, i: (b, i, 0))
    return pl.pallas_call(
        _out_proj_kernel,
        grid=(B, S // tm),
        in_specs=[tok(D_MODEL), tok(ATTN_WIDTH), _lru_rows_spec(tm),
                  _const_spec((1, ATTN_WIDTH)), _const_spec((1, LRU_WIDTH)),
                  _const_spec((ATTN_WIDTH, D_MODEL)), _const_spec((LRU_WIDTH, D_MODEL))],
        out_specs=tok(D_MODEL),
        out_shape=jax.ShapeDtypeStruct(x.shape, _F32),
        compiler_params=_params(("parallel", "parallel")),
        name="out_proj",
    )(x, attn, lru, ga, gl, w_out_a, w_out_l)


def _ffn_kernel(x_ref, g_ref, wg_ref, wu_ref, wd_ref, y_ref, h_scr):
    @pl.when(pl.program_id(2) == 0)
    def _():
        x = x_ref[0]
        h_scr[...] = _rms(x, g_ref[...]).astype(_BF16)
        y_ref[0] = x

    h = h_scr[...]
    gate = jnp.dot(h, wg_ref[...], preferred_element_type=_F32)
    up = jnp.dot(h, wu_ref[...], preferred_element_type=_F32)
    act = (jax.nn.silu(gate) * up).astype(_BF16)
    y_ref[0] += jnp.dot(act, wd_ref[...], preferred_element_type=_F32)


def _ffn(x, g, w_gate, w_up, w_down, tm, tf):
    B, S, _ = x.shape
    tok = pl.BlockSpec((1, tm, D_MODEL), lambda b, i, j: (b, i, 0))
    return pl.pallas_call(
        _ffn_kernel,
        grid=(B, S // tm, D_FF // tf),
        in_specs=[tok, pl.BlockSpec((1, D_MODEL), lambda b, i, j: (0, 0)),
                  pl.BlockSpec((D_MODEL, tf), lambda b, i, j: (0, j)),
                  pl.BlockSpec((D_MODEL, tf), lambda b, i, j: (0, j)),
                  pl.BlockSpec((tf, D_MODEL), lambda b, i, j: (j, 0))],
        out_specs=tok,
        out_shape=jax.ShapeDtypeStruct(x.shape, _F32),
        scratch_shapes=[pltpu.VMEM((tm, D_MODEL), _BF16)],
        compiler_params=_params(("parallel", "parallel", "arbitrary")),
        name="ffn",
    )(x, g, w_gate, w_up, w_down)


def _swap_halves(w):
    half = ROPE // 2
    return jnp.concatenate([-w[..., half:], w[..., :half]], axis=-1)


def _swap_gain(g):
    half = ROPE // 2
    return jnp.concatenate([g[half:], g[:half]])


def _gain_rows(g):
    pad = jnp.zeros((LANES - ROPE,), _F32)
    pe = g[NOPE:]
    return jnp.stack([g[:NOPE], jnp.concatenate([pe, pad]),
                      jnp.concatenate([_swap_gain(pe), pad])])


def _prepare(attn_norm_g, w_in, q_a_norm_g, w_uq, kv_a_norm_g, w_ukv, q_norm_g, k_norm_g,
             conv_w, conv_b, w_rg_r, b_rg_r, w_rg_i, b_rg_i, lru_lambda,
             attn_out_norm_g, lru_out_norm_g, w_out, ffn_norm_g, w_gate, w_up, w_down):
    k_pe = w_in[:, OFF_KV:OFF_KV + ROPE]
    w_head = jnp.concatenate([w_in[:, :OFF_KV + ROPE], _swap_halves(k_pe)], axis=1).astype(_BF16)
    w_lru = w_in[:, OFF_KV + ROPE:].astype(_BF16)
    wq = w_uq.reshape(Q_RANK, N_HEADS, QK_DIM)
    w_uq_ext = jnp.concatenate([wq, _swap_halves(wq[..., NOPE:])], axis=-1)
    w_uq_ext = w_uq_ext.reshape(Q_RANK, N_HEADS * HEAD_PAD).astype(_BF16)
    wkv = w_ukv.reshape(KV_RANK, N_HEADS, NOPE + V_DIM)
    w_uk = wkv[..., :NOPE].reshape(KV_RANK, N_HEADS * NOPE).astype(_BF16)
    w_uvt = wkv[..., NOPE:].reshape(KV_RANK, ATTN_WIDTH).T.astype(_BF16)
    wg = (0.5 * jnp.concatenate([w_rg_r, w_rg_i], axis=-1)).astype(_BF16)
    bg = 0.5 * jnp.concatenate([b_rg_r, b_rg_i], axis=-1)[:, :, None, :]
    lam = lru_lambda.reshape(2, LRU_BLOCKS, 1, LRU_BD)
    return dict(
        g_in=attn_norm_g[None], w_head=w_head, w_lru=w_lru, g_qa=q_a_norm_g[None], w_uq_ext=w_uq_ext,
        g_kva=kv_a_norm_g[None], w_uk=w_uk, w_uvt=w_uvt,
        gq=_gain_rows(q_norm_g), gk=_gain_rows(k_norm_g),
        conv_w=conv_w.reshape(CONV_W, LRU_BLOCKS, LRU_BD), conv_b=conv_b.reshape(LRU_BLOCKS, LRU_BD),
        wg=wg, bg=bg, lam=lam,
        ga=attn_out_norm_g[None], gl=lru_out_norm_g[None],
        w_out_a=w_out[:ATTN_WIDTH].astype(_BF16), w_out_l=w_out[ATTN_WIDTH:].astype(_BF16),
        g_ffn=ffn_norm_g[None], w_gate=w_gate.astype(_BF16), w_up=w_up.astype(_BF16),
        w_down=w_down.astype(_BF16))


def _rope_tables(S):
    half = ROPE // 2
    inv_freq = ROPE_BASE ** (-2.0 * np.arange(half, dtype=np.float64) / ROPE)
    ang = np.arange(S, dtype=np.float64)[:, None] * inv_freq[None, :]
    pad = np.zeros((S, LANES - ROPE))
    cos = np.concatenate([np.cos(ang), np.cos(ang), pad], axis=1)
    sin = np.concatenate([np.sin(ang), np.sin(ang), pad], axis=1)
    return jnp.asarray(cos, _F32), jnp.asarray(sin, _F32)


def _tiles(S):
    attn_k = min(512, S // 2)
    attn_q = min(1024 if S // attn_k <= 8 else 512, S)
    return dict(in_proj=min(512, S), attn_q=attn_q, attn_k=attn_k,
                attn_span=min(ATTN_SPAN, S),
                lru=min(1024, S), out_proj=min(512, S), ffn=min(1024, S), ffn_cols=512)


def _encoder_layer(x, p):
    B, S, _ = x.shape
    t = _tiles(S)
    cos, sin = _rope_tables(S)
    q, k, vt, ux, ug = _in_proj(x, cos, sin, p["g_in"], p["w_head"], p["w_lru"], p["g_qa"], p["w_uq_ext"],
                               p["g_kva"], p["w_uk"], p["w_uvt"], p["gq"], p["gk"], t["in_proj"])
    attn = _attention(q, k, vt, t["attn_q"], t["attn_k"], t["attn_span"])
    shape4 = (B, S, LRU_BLOCKS, LRU_BD)
    lru = _lru(ux.reshape(shape4), ug.reshape(shape4), p["conv_w"], p["conv_b"],
               p["wg"], p["bg"], p["lam"], t["lru"]).reshape(B, S * LRU_BLOCKS, LRU_BD)
    x1 = _out_proj(x, attn, lru, p["ga"], p["gl"], p["w_out_a"], p["w_out_l"], t["out_proj"])
    return _ffn(x1, p["g_ffn"], p["w_gate"], p["w_up"], p["w_down"], t["ffn"], t["ffn_cols"])


def kernel(x_prompt, x_sample, attn_norm_g, w_in, q_a_norm_g, w_uq, kv_a_norm_g, w_ukv, q_norm_g, k_norm_g, conv_w, conv_b, w_rg_r, b_rg_r, w_rg_i, b_rg_i, lru_lambda, attn_out_norm_g, lru_out_norm_g, w_out, ffn_norm_g, w_gate, w_up, w_down):
    depth = attn_norm_g.shape[0]
    weights = (attn_norm_g, w_in, q_a_norm_g, w_uq, kv_a_norm_g, w_ukv, q_norm_g, k_norm_g,
               conv_w, conv_b, w_rg_r, b_rg_r, w_rg_i, b_rg_i, lru_lambda,
               attn_out_norm_g, lru_out_norm_g, w_out, ffn_norm_g, w_gate, w_up, w_down)
    y_prompt, y_sample = x_prompt, x_sample
    for layer in range(depth):
        p = _prepare(*(w[layer] for w in weights))
        y_prompt = _encoder_layer(y_prompt, p)
        y_sample = _encoder_layer(y_sample, p)
    return (y_prompt, y_sample)
```
